```python
import functools
import jax, jax.numpy as jnp
from jax import lax
import numpy as np

D_MODEL = 4096
BATCH = 4
SEQ = 2048
DEPTH = 1
DEC_BATCH = 128
DEC_SEQ = 4
PAST_LEN = 16384
PAGE_SIZE = 128

D_MIX = D_MODEL
GLA_WIDTH = D_MIX // 2
SSD_WIDTH = D_MIX - GLA_WIDTH
GLA_HEADS = 4
GLA_DV = GLA_WIDTH // GLA_HEADS
GLA_DK = GLA_DV // 2
GLA_QK = GLA_HEADS * GLA_DK
GATE_RANK = 16
GATE_NORM = 16.0
SSD_HEAD_DIM = 64
SSD_HEADS = SSD_WIDTH // SSD_HEAD_DIM
SSD_GROUPS = 8
HEADS_PER_GROUP = SSD_HEADS // SSD_GROUPS
D_STATE = 128
CONV_W = 4
CONV_DIM = SSD_WIDTH + 2 * SSD_GROUPS * D_STATE
D_FF = 4 * D_MODEL
N_META = 16
CHUNK = 64
SPLIT_SIZES = (GLA_QK, GLA_QK, GLA_WIDTH, GLA_WIDTH, GATE_RANK, SSD_WIDTH, CONV_DIM, SSD_HEADS)
D_IN = GLA_QK * 2 + GLA_WIDTH * 2 + GATE_RANK + SSD_WIDTH + CONV_DIM + SSD_HEADS
ALPHA = (2.0 * DEPTH) ** 0.25
BETA = (8.0 * DEPTH) ** -0.25
EPS = 1e-5

kernel_name = "hymba_gla_ssd_deepnorm_step"


def _layer_norm(x, g, b):
    xf = x.astype(jnp.float32)
    mu = jnp.mean(xf, axis=-1, keepdims=True)
    var = jnp.mean(jnp.square(xf - mu), axis=-1, keepdims=True)
    return ((xf - mu) * lax.rsqrt(var + EPS) * g.astype(jnp.float32) + b.astype(jnp.float32)).astype(x.dtype)


def _rms(xf):
    return xf * lax.rsqrt(jnp.mean(jnp.square(xf), axis=-1, keepdims=True) + EPS)


def _gla_chunk(S, q, k, v, g):
    T = q.shape[1]
    b = jnp.cumsum(g, axis=1)
    mask = jnp.tril(jnp.ones((T, T), dtype=bool))
    diff = b[:, :, None] - b[:, None, :]
    decay = jnp.exp(jnp.where(mask[None, :, :, None, None], diff, -jnp.inf))
    scores = jnp.einsum('bthk,bshk,btshk->bhts', q, k, decay)
    o = jnp.einsum('bhts,bshv->bthv', scores, v) + jnp.einsum('bthk,bhkv->bthv', q * jnp.exp(b), S)
    b_last = b[:, -1]
    S_new = jnp.exp(b_last)[..., None] * S + jnp.einsum(
        'bshk,bshv->bhkv', k * jnp.exp(b_last[:, None] - b), v)
    return S_new, o


def _ssd_chunk(h, x, bm, cm, dt, a):
    T = x.shape[1]
    cum = jnp.cumsum(dt * a, axis=1)
    mask = jnp.tril(jnp.ones((T, T), dtype=bool))
    lmat = jnp.exp(jnp.where(mask[None, :, :, None, None], cum[:, :, None] - cum[:, None], -jnp.inf))
    cb = jnp.einsum('btgn,bsgn->btsg', cm, bm)
    y = jnp.einsum('btsg,btsgj,bsgj,bsgjp->btgjp', cb, lmat, dt, x)
    y = y + jnp.einsum('btgn,bgjpn->btgjp', cm, h) * jnp.exp(cum)[..., None]
    c_last = cum[:, -1]
    h_new = jnp.exp(c_last)[..., None, None] * h + jnp.einsum(
        'bsgn,bsgj,bsgjp->bgjpn', bm, jnp.exp(c_last[:, None] - cum) * dt, x)
    return h_new, y


def _run_chunked(step, state, xs, lead):
    T = xs[0].shape[1]
    outs = []
    if lead > 0:
        state, o = step(state, *[a[:, :lead] for a in xs])
        outs.append(o)
    n_full = (T - lead) // CHUNK
    if n_full > 0:
        blocks = tuple(
            jnp.moveaxis(a[:, lead:lead + n_full * CHUNK].reshape(a.shape[0], n_full, CHUNK, *a.shape[2:]), 1, 0)
            for a in xs)
        state, o = lax.scan(lambda s, blk: step(s, *blk), state, blocks)
        o = jnp.moveaxis(o, 0, 1)
        outs.append(o.reshape(o.shape[0], n_full * CHUNK, *o.shape[3:]))
    start = lead + n_full * CHUNK
    if start < T:
        state, o = step(state, *[a[:, start:] for a in xs])
        outs.append(o)
    return state, jnp.concatenate(outs, axis=1)


def _layer(x, gla_s, ssm_h, conv_buf, lead, w_in, w_gk_up, b_gk, gla_norm_w, conv_w, conv_b,
           dt_bias, a_log, d_skip, ssd_norm_w, w_out, ln1_g, ln1_b, w_up, w_down, ln2_g, ln2_b):
    f32 = jnp.float32
    B, T, _ = x.shape
    u = x @ w_in
    pts, acc = [], 0
    for s in SPLIT_SIZES[:-1]:
        acc += s
        pts.append(acc)
    q, k, v, r, gk_low, z, xbc, dt_raw = jnp.split(u, pts, axis=-1)

    qh = q.reshape(B, T, GLA_HEADS, GLA_DK).astype(f32) * (GLA_DK ** -0.5)
    kh = k.reshape(B, T, GLA_HEADS, GLA_DK).astype(f32)
    vh = v.reshape(B, T, GLA_HEADS, GLA_DV).astype(f32)
    g = jax.nn.log_sigmoid((gk_low @ w_gk_up + b_gk).astype(f32)) / GATE_NORM
    g = g.reshape(B, T, GLA_HEADS, GLA_DK)
    gla_new, o = _run_chunked(_gla_chunk, gla_s.astype(f32), (qh, kh, vh, g), lead)
    o = _rms(o) * gla_norm_w.astype(f32) * jax.nn.silu(r.reshape(B, T, GLA_HEADS, GLA_DV).astype(f32))
    o = o.reshape(B, T, GLA_WIDTH)

    xbc_full = jnp.concatenate([conv_buf.astype(xbc.dtype), xbc], axis=1)
    conv_new = xbc_full[:, -(CONV_W - 1):]
    xc = lax.conv_general_dilated(xbc_full, conv_w[:, None, :].astype(xbc.dtype), window_strides=(1,),
                                  padding='VALID', dimension_numbers=('NWC', 'WIO', 'NWC'),
                                  feature_group_count=CONV_DIM)
    xc = jax.nn.silu((xc + conv_b).astype(f32))
    xs, bm, cm = jnp.split(xc, [SSD_WIDTH, SSD_WIDTH + SSD_GROUPS * D_STATE], axis=-1)
    xs = xs.reshape(B, T, SSD_GROUPS, HEADS_PER_GROUP, SSD_HEAD_DIM)
    bm = bm.reshape(B, T, SSD_GROUPS, D_STATE)
    cm = cm.reshape(B, T, SSD_GROUPS, D_STATE)
    dt = jax.nn.softplus(dt_raw.astype(f32) + dt_bias.astype(f32)).reshape(B, T, SSD_GROUPS, HEADS_PER_GROUP)
    a = -jnp.exp(a_log.astype(f32)).reshape(SSD_GROUPS, HEADS_PER_GROUP)
    h0 = ssm_h.astype(f32).reshape(B, SSD_GROUPS, HEADS_PER_GROUP, SSD_HEAD_DIM, D_STATE)
    ssm_new, y = _run_chunked(functools.partial(_ssd_chunk, a=a), h0, (xs, bm, cm, dt), lead)
    y = y + d_skip.astype(f32).reshape(SSD_GROUPS, HEADS_PER_GROUP)[..., None] * xs
    y = y.reshape(B, T, SSD_WIDTH) * jax.nn.silu(z.astype(f32))
    y = _rms(y.reshape(B, T, SSD_GROUPS, SSD_WIDTH // SSD_GROUPS)).reshape(B, T, SSD_WIDTH)
    y = y * ssd_norm_w.astype(f32)

    mix = jnp.concatenate([o, y], axis=-1).astype(x.dtype) @ w_out
    h = _layer_norm(ALPHA * x + mix, ln1_g, ln1_b)
    f = jnp.square(jax.nn.relu(h @ w_up)) @ w_down
    out = _layer_norm(ALPHA * h + f, ln2_g, ln2_b)
    return (out, gla_new.astype(x.dtype),
            ssm_new.reshape(B, SSD_HEADS, SSD_HEAD_DIM, D_STATE).astype(x.dtype), conv_new.astype(x.dtype))


def setup_inputs(seed: int = 0) -> dict:
    key = jax.random.key(seed)
    ks = jax.random.split(key, 26)
    nrm = jax.random.normal
    f32 = jnp.float32
    dt = jnp.exp(jax.random.uniform(ks[12], (DEPTH, SSD_HEADS), f32) * (np.log(0.1) - np.log(0.001)) + np.log(0.001))
    return {
        "x_prompt": nrm(ks[0], (BATCH, SEQ, D_MODEL), f32),
        "x_sample": nrm(ks[1], (DEC_BATCH, DEC_SEQ, D_MODEL), f32),
        "state_gla": 0.5 * nrm(ks[2], (DEPTH, DEC_BATCH, GLA_HEADS, GLA_DK, GLA_DV), f32),
        "state_ssm": 0.1 * nrm(ks[3], (DEPTH, DEC_BATCH, SSD_HEADS, SSD_HEAD_DIM, D_STATE), f32),
        "state_conv": nrm(ks[4], (DEPTH, DEC_BATCH, CONV_W - 1, CONV_DIM), f32),
        "meta_tokens": nrm(ks[5], (N_META, D_MODEL), f32),
        "w_in": nrm(ks[6], (DEPTH, D_MODEL, D_IN), f32) * D_MODEL ** -0.5,
        "w_gk_up": nrm(ks[7], (DEPTH, GATE_RANK, GLA_QK), f32) * GATE_RANK ** -0.5,
        "b_gk": 0.1 * nrm(ks[8], (DEPTH, GLA_QK), f32),
        "gla_norm_w": 1.0 + 0.02 * nrm(ks[9], (DEPTH, GLA_DV), f32),
        "conv_w": nrm(ks[10], (DEPTH, CONV_W, CONV_DIM), f32) * CONV_W ** -0.5,
        "conv_b": 0.02 * nrm(ks[11], (DEPTH, CONV_DIM), f32),
        "dt_bias": dt + jnp.log(-jnp.expm1(-dt)),
        "a_log": jnp.log(jax.random.uniform(ks[13], (DEPTH, SSD_HEADS), f32, 1.0, 16.0)),
        "d_skip": 1.0 + 0.1 * nrm(ks[14], (DEPTH, SSD_HEADS), f32),
        "ssd_norm_w": 1.0 + 0.02 * nrm(ks[15], (DEPTH, SSD_WIDTH), f32),
        "w_out": nrm(ks[16], (DEPTH, D_MIX, D_MODEL), f32) * (D_MIX ** -0.5) * BETA,
        "ln1_g": 1.0 + 0.02 * nrm(ks[17], (DEPTH, D_MODEL), f32),
        "ln1_b": 0.02 * nrm(ks[18], (DEPTH, D_MODEL), f32),
        "w_up": nrm(ks[19], (DEPTH, D_MODEL, D_FF), f32) * D_MODEL ** -0.5,
        "w_down": nrm(ks[20], (DEPTH, D_FF, D_MODEL), f32) * (D_FF ** -0.5) * BETA,
        "ln2_g": 1.0 + 0.02 * nrm(ks[21], (DEPTH, D_MODEL), f32),
        "ln2_b": 0.02 * nrm(ks[22], (DEPTH, D_MODEL), f32),
    }


def reference(x_prompt, x_sample, state_gla, state_ssm, state_conv, meta_tokens, w_in, w_gk_up, b_gk,
              gla_norm_w, conv_w, conv_b, dt_bias, a_log, d_skip, ssd_norm_w, w_out, ln1_g, ln1_b,
              w_up, w_down, ln2_g, ln2_b):
    bp = x_prompt.shape[0]
    dt_ = x_prompt.dtype
    meta = jnp.broadcast_to(meta_tokens.astype(dt_)[None], (bp, N_META, D_MODEL))
    xp = jnp.concatenate([meta, x_prompt], axis=1)
    xs = x_sample
    gla_p, ssm_p, conv_p, gla_s, ssm_s, conv_s = [], [], [], [], [], []
    for l in range(DEPTH):
        p = (w_in[l], w_gk_up[l], b_gk[l], gla_norm_w[l], conv_w[l], conv_b[l], dt_bias[l], a_log[l],
             d_skip[l], ssd_norm_w[l], w_out[l], ln1_g[l], ln1_b[l], w_up[l], w_down[l], ln2_g[l], ln2_b[l])
        zg = jnp.zeros((bp, GLA_HEADS, GLA_DK, GLA_DV), dt_)
        zs = jnp.zeros((bp, SSD_HEADS, SSD_HEAD_DIM, D_STATE), dt_)
        zc = jnp.zeros((bp, CONV_W - 1, CONV_DIM), dt_)
        xp, g1, s1, c1 = _layer(xp, zg, zs, zc, N_META, *p)
        xs, g2, s2, c2 = _layer(xs, state_gla[l], state_ssm[l], state_conv[l], 0, *p)
        gla_p.append(g1); ssm_p.append(s1); conv_p.append(c1)
        gla_s.append(g2); ssm_s.append(s2); conv_s.append(c2)
    y_prompt = xp[:, N_META:]
    return (y_prompt, xs, jnp.stack(gla_p), jnp.stack(ssm_p), jnp.stack(conv_p),
            jnp.stack(gla_s), jnp.stack(ssm_s), jnp.stack(conv_s))
```

```python
import functools

import numpy as np
import jax
import jax.numpy as jnp
from jax import lax
from jax.experimental import pallas as pl
from jax.experimental.pallas import tpu as pltpu

F32 = jnp.float32
BF16 = jnp.bfloat16

GLA_HEADS = 4
GLA_DK = 256
GLA_DV = 512
GLA_QK = GLA_HEADS * GLA_DK
GLA_WIDTH = GLA_HEADS * GLA_DV
GATE_RANK = 16
GATE_NORM = 16.0
SSD_GROUPS = 8
HEADS_PER_GROUP = 4
SSD_HEAD_DIM = 64
SSD_HEADS = SSD_GROUPS * HEADS_PER_GROUP
GROUP_WIDTH = HEADS_PER_GROUP * SSD_HEAD_DIM
SSD_WIDTH = SSD_GROUPS * GROUP_WIDTH
D_STATE = 128
CONV_W = 4
CONV_DIM = SSD_WIDTH + 2 * SSD_GROUPS * D_STATE
N_META = 16
CHUNK = 64
EPS = 1e-5
DEPTH = 1
ALPHA = (2.0 * DEPTH) ** 0.25

U_K = GLA_QK
U_V = 2 * GLA_QK
U_R = U_V + GLA_WIDTH
U_Z = U_R + GLA_WIDTH
U_X = U_Z + SSD_WIDTH
U_B = U_X + SSD_WIDTH
U_C = U_B + SSD_GROUPS * D_STATE
U_MAIN = U_C + SSD_GROUPS * D_STATE
LANE = 128
SM_DT = GATE_RANK

VMEM_LIMIT = 56 * 1024 * 1024


def _mm(a, b):
    return jnp.dot(a, b, preferred_element_type=F32)


def _nt(a, b):
    return lax.dot_general(a, b, (((1,), (1,)), ((), ())), preferred_element_type=F32)


def _split2(x):
    hi = x.astype(BF16)
    lo = (x - hi.astype(F32)).astype(BF16)
    return hi, lo


def _split3(x):
    hi = x.astype(BF16)
    r1 = x - hi.astype(F32)
    mid = r1.astype(BF16)
    lo = (r1 - mid.astype(F32)).astype(BF16)
    return hi, mid, lo


def _row(x, i):
    rows = lax.broadcasted_iota(jnp.int32, x.shape, 0)
    return jnp.sum(jnp.where(rows == i, x, 0.0), axis=0, keepdims=True)


def _pad_rows(x, n):
    t = x.shape[0]
    if t == n:
        return x
    return jnp.concatenate([x, jnp.zeros((n - t, x.shape[1]), x.dtype)], axis=0)


def _silu(x):
    return x * (1.0 / (1.0 + jnp.exp(-x)))


def _softplus(x):
    return jnp.maximum(x, 0.0) + jnp.log(1.0 + jnp.exp(-jnp.abs(x)))


def _eye_bf16(n):
    ri = lax.broadcasted_iota(jnp.int32, (n, n), 0)
    ci = lax.broadcasted_iota(jnp.int32, (n, n), 1)
    return jnp.where(ri == ci, 1.0, 0.0).astype(BF16)


def _seg_masks(tb, width, segs, lane_mod):
    ri = lax.broadcasted_iota(jnp.int32, (tb, width), 0)
    ci = lax.broadcasted_iota(jnp.int32, (tb, width), 1)
    if lane_mod is not None:
        ci = ci & lane_mod
    same = None
    for lo, hi in segs:
        m = (ri >= lo) & (ri < hi) & (ci >= lo) & (ci < hi)
        same = m if same is None else (same | m)
    return same, same & (ci <= ri)


def _transpose_operand(rows_bf16, decay_row, tb):
    w = rows_bf16.shape[1]
    d16 = jnp.broadcast_to(decay_row, (16, w))
    hi, mid, lo = _split3(d16)
    ridx = lax.broadcasted_iota(jnp.int32, (16, w), 0)
    zero = jnp.zeros((16, w), F32)
    e16 = jnp.where(ridx == 0, hi.astype(F32),
                    jnp.where(ridx == 1, mid.astype(F32),
                              jnp.where(ridx == 2, lo.astype(F32), zero))).astype(BF16)
    pieces = [rows_bf16]
    if tb < CHUNK:
        pieces.append(jnp.zeros((CHUNK - tb, w), BF16))
    pieces += [e16, jnp.zeros((LANE - CHUNK - 16, w), BF16)]
    return jnp.concatenate(pieces, axis=0)


def _transposed_with_decay(eye, rows_bf16, decay_row, tb):
    t = _nt(eye, _transpose_operand(rows_bf16, decay_row, tb))
    lane = lax.broadcasted_iota(jnp.int32, t.shape, 1)
    col = jnp.sum(jnp.where(lane >= CHUNK, t, 0.0), axis=1, keepdims=True)
    return t.astype(BF16), col


def _gla_chunk(q, k, v, g, segs, states, eye):
    tb, dk = q.shape
    full = len(segs) == 1 and segs[0] == (0, tb)
    _, causal = _seg_masks(tb, tb, segs, None)
    tri = jnp.where(causal, 1.0, 0.0).astype(BF16)
    g_hi, g_lo = _split2(g)
    bc = _mm(tri, g_hi) + _mm(tri, g_lo)
    rows = lax.broadcasted_iota(jnp.int32, (tb, 1), 0)
    anchor = jnp.zeros_like(bc)
    blast = jnp.zeros_like(bc)
    last_rows = []
    for lo, hi in segs:
        inseg = (rows >= lo) & (rows < hi)
        m_row = _row(bc, (lo + hi) // 2 - 1)
        l_row = _row(bc, hi - 1)
        last_rows.append(l_row)
        anchor = jnp.where(inseg, m_row, anchor)
        blast = jnp.where(inseg, l_row, blast)
    qs = q * (dk ** -0.5)
    qt = (qs * jnp.exp(bc - anchor)).astype(BF16)
    kt = _pad_rows((k * jnp.exp(anchor - bc)).astype(BF16), LANE)
    _, causal_p = _seg_masks(tb, LANE, segs, None)
    p = jnp.where(causal_p, _nt(qt, kt), 0.0)
    vb = _pad_rows(v.astype(BF16), LANE)
    o = _mm(p.astype(BF16), vb)
    qb = (qs * jnp.exp(bc)).astype(BF16)
    kd = k * jnp.exp(blast - bc)
    new_states = []
    for (lo, hi), l_row, s in zip(segs, last_rows, states):
        inseg = (rows >= lo) & (rows < hi)
        oi = _mm(qb, s.astype(BF16))
        o = o + (oi if full else jnp.where(inseg, oi, 0.0))
        kd_s = kd if full else jnp.where(inseg, kd, 0.0)
        kd_t, e_col = _transposed_with_decay(eye, kd_s.astype(BF16), jnp.exp(l_row), tb)
        new_states.append(e_col * s + _mm(kd_t, vb))
    return o, new_states


def _gla_kernel(q_ref, k_ref, v_ref, r_ref, sm_ref, wg_ref, bg_ref, gnw_ref, s0_ref,
                og_ref, s_ref, *, tb, segs, carry):
    rows = q_ref.shape[0]
    eye = _eye_bf16(GLA_DK)
    if carry:
        @pl.when(pl.program_id(2) == 0)
        def _():
            s_ref[...] = s0_ref[...]
    gpre = _mm(sm_ref[...].astype(BF16), wg_ref[...]) + bg_ref[...]
    g = -_softplus(-gpre) * (1.0 / GATE_NORM)
    outs = []
    for c in range(rows // tb):
        sl = pl.ds(c * tb, tb)
        if carry:
            states = [s_ref[0, 0]]
        else:
            states = [s0_ref[i, 0] for i in range(len(segs))]
        o, new_states = _gla_chunk(q_ref[sl, :], k_ref[sl, :], v_ref[sl, :], g[c * tb:(c + 1) * tb],
                                   segs, states, eye)
        if carry:
            s_ref[0, 0] = new_states[0]
        else:
            for i, s in enumerate(new_states):
                s_ref[i, 0] = s
        outs.append(o)
    o = outs[0] if len(outs) == 1 else jnp.concatenate(outs, axis=0)
    o = o * lax.rsqrt(jnp.mean(jnp.square(o), axis=-1, keepdims=True) + EPS)
    o = o * gnw_ref[...] * _silu(r_ref[...])
    og_ref[...] = o.astype(og_ref.dtype)


def _gla_scan(u, usm, wg, bg, gnw, s0, *, n_seq, seq_rows, block_rows, tb, segs, carry):
    m = u.shape[0]
    kern = functools.partial(_gla_kernel, tb=tb, segs=segs, carry=carry)
    if carry:
        nb = seq_rows // block_rows
        grid = (n_seq, GLA_HEADS, nb)
        s0_mul = 1 if s0.shape[0] == n_seq else 0
        row = lambda b, h, c: b * nb + c
        spec = lambda width, off: pl.BlockSpec((block_rows, width), lambda b, h, c: (row(b, h, c), off + h))
        in_specs = [
            spec(GLA_DK, 0), spec(GLA_DK, U_K // GLA_DK), spec(GLA_DV, U_V // GLA_DV), spec(GLA_DV, U_R // GLA_DV),
            pl.BlockSpec((block_rows, LANE), lambda b, h, c: (row(b, h, c), 0)),
            pl.BlockSpec((LANE, GLA_DK), lambda b, h, c: (0, h)),
            pl.BlockSpec((1, GLA_DK), lambda b, h, c: (0, h)),
            pl.BlockSpec((1, GLA_DV), lambda b, h, c: (0, 0)),
            pl.BlockSpec((1, 1, GLA_DK, GLA_DV), lambda b, h, c: (b * s0_mul, h, 0, 0)),
        ]
        out_specs = [
            pl.BlockSpec((block_rows, GLA_DV), lambda b, h, c: (row(b, h, c), h)),
            pl.BlockSpec((1, 1, GLA_DK, GLA_DV), lambda b, h, c: (b, h, 0, 0)),
        ]
        sem = ("parallel", "parallel", "arbitrary")
    else:
        per = len(segs)
        grid = (m // block_rows, GLA_HEADS)
        spec = lambda width, off: pl.BlockSpec((block_rows, width), lambda i, h: (i, off + h))
        in_specs = [
            spec(GLA_DK, 0), spec(GLA_DK, U_K // GLA_DK), spec(GLA_DV, U_V // GLA_DV), spec(GLA_DV, U_R // GLA_DV),
            pl.BlockSpec((block_rows, LANE), lambda i, h: (i, 0)),
            pl.BlockSpec((LANE, GLA_DK), lambda i, h: (0, h)),
            pl.BlockSpec((1, GLA_DK), lambda i, h: (0, h)),
            pl.BlockSpec((1, GLA_DV), lambda i, h: (0, 0)),
            pl.BlockSpec((per, 1, GLA_DK, GLA_DV), lambda i, h: (i, h, 0, 0)),
        ]
        out_specs = [
            pl.BlockSpec((block_rows, GLA_DV), lambda i, h: (i, h)),
            pl.BlockSpec((per, 1, GLA_DK, GLA_DV), lambda i, h: (i, h, 0, 0)),
        ]
        sem = ("parallel", "parallel")
    return pl.pallas_call(
        kern, grid=grid, in_specs=in_specs, out_specs=out_specs,
        out_shape=[jax.ShapeDtypeStruct((m, GLA_WIDTH), BF16),
                   jax.ShapeDtypeStruct((n_seq, GLA_HEADS, GLA_DK, GLA_DV), F32)],
        compiler_params=pltpu.CompilerParams(dimension_semantics=sem, vmem_limit_bytes=VMEM_LIMIT),
        name="gla_scan_carry" if carry else "gla_scan_multi",
    )(u, u, u, u, usm, wg, bg, gnw, s0)


def _ssd_chunk(x, bm, cm, dt, dta, expand, segs, states, eye):
    tb = x.shape[0]
    gw = x.shape[1]
    full = len(segs) == 1 and segs[0] == (0, tb)
    same, causal = _seg_masks(tb, tb, segs, None)
    tri = jnp.where(causal, 1.0, 0.0).astype(BF16)
    ones_seg = jnp.where(same, 1.0, 0.0).astype(BF16)
    parts = [_mm(p, expand).astype(BF16) for p in _split3(dta)]
    cb = _mm(tri, parts[0]) + _mm(tri, parts[1]) + _mm(tri, parts[2])
    ri = lax.broadcasted_iota(jnp.int32, (tb, gw), 0)
    li = lax.broadcasted_iota(jnp.int32, (tb, gw), 1)
    diag = jnp.where((li & (CHUNK - 1)) == ri, cb, 0.0)
    d3 = _split3(diag)
    rw = _mm(ones_seg, d3[0]) + _mm(ones_seg, d3[1]) + _mm(ones_seg, d3[2])
    _, causal_l = _seg_masks(tb, gw, segs, CHUNK - 1)
    decay = jnp.exp(jnp.where(causal_l, cb - rw, -jnp.inf))
    bm_b = bm.astype(BF16)
    cm_b = cm.astype(BF16)
    b64 = _pad_rows(bm_b, CHUNK)
    cb4 = _nt(cm_b, jnp.concatenate([b64] * HEADS_PER_GROUP, axis=0))
    dt_hi, dt_lo = _split2(dt)
    xdt = x * (_mm(dt_hi, expand) + _mm(dt_lo, expand))
    x64 = _pad_rows(xdt, CHUNK)
    r2 = lax.broadcasted_iota(jnp.int32, (gw, gw), 0)
    l2 = lax.broadcasted_iota(jnp.int32, (gw, gw), 1)
    bd = jnp.where((r2 >> 6) == (l2 >> 6), jnp.concatenate([x64] * HEADS_PER_GROUP, axis=0), 0.0)
    y = _mm((cb4 * decay).astype(BF16), bd.astype(BF16))
    ecb = jnp.exp(cb)
    rows = lax.broadcasted_iota(jnp.int32, (tb, 1), 0)
    bpad = _pad_rows(bm_b, LANE)
    new_states = []
    for (lo, hi), h in zip(segs, states):
        inseg = (rows >= lo) & (rows < hi)
        yi = _nt(cm_b, h.astype(BF16)) * ecb
        y = y + (yi if full else jnp.where(inseg, yi, 0.0))
        cl = _row(cb, hi - 1)
        xw = xdt * jnp.exp(cl - cb)
        if not full:
            xw = jnp.where(inseg, xw, 0.0)
        xw_t, e_col = _transposed_with_decay(eye, xw.astype(BF16), jnp.exp(cl), tb)
        new_states.append(e_col * h + _mm(xw_t, bpad))
    return y, new_states


def _ssd_kernel(x_ref, b_ref, c_ref, z_ref, sm_ref, cwx_ref, cwb_ref, cwc_ref, cbx_ref, cbb_ref, cbc_ref,
                dtb_ref, alog_ref, ex_ref, dsk_ref, nw_ref, csx_ref, csb_ref, csc_ref, h0_ref,
                ys_ref, h_ref, xpad, *scratch, tb, segs, carry):
    rows = x_ref.shape[0]
    gw = GROUP_WIDTH
    eye = _eye_bf16(gw)
    lanes = ((0, gw), (gw, gw + D_STATE), (gw + D_STATE, gw + 2 * D_STATE))
    ins = (x_ref, b_ref, c_ref)
    css = (csx_ref, csb_ref, csc_ref)

    def conv_rows(n):
        acc = jnp.concatenate([cbx_ref[...], cbb_ref[...], cbc_ref[...]], axis=1)
        acc = jnp.broadcast_to(acc, (n, acc.shape[1]))
        for d in range(CONV_W):
            wrow = jnp.concatenate([w[pl.ds(CONV_W - 1 - d, 1), :] for w in (cwx_ref, cwb_ref, cwc_ref)], axis=1)
            acc = acc + wrow * xpad[pl.ds(8 - d, n), :]
        return acc

    if carry:
        @pl.when(pl.program_id(2) == 0)
        def _():
            h_ref[...] = h0_ref[...]
            for (lo, hi), cs in zip(lanes, css):
                xpad[pl.ds(0, 8), lo:hi] = cs[0]
        for (lo, hi), src in zip(lanes, ins):
            xpad[pl.ds(8, rows), lo:hi] = src[...]
        xc = _silu(conv_rows(rows))
        xpad[pl.ds(0, 8), :] = xpad[pl.ds(rows, 8), :]
    else:
        cres, xcs = scratch
        nseq = len(segs)
        xpad[...] = jnp.zeros(xpad.shape, F32)
        for i in range(nseq):
            for (lo, hi), src, cs in zip(lanes, ins, css):
                xpad[pl.ds(8 + 8 * i + 1, CONV_W - 1), lo:hi] = cs[i]
                xpad[pl.ds(8 + 8 * i + 4, 4), lo:hi] = src[pl.ds(4 * i, 4), :]
        cres[...] = conv_rows(8 * nseq)
        for i in range(nseq):
            xcs[pl.ds(4 * i, 4), :] = cres[pl.ds(8 * i + 4, 4), :]
        xc = _silu(xcs[...])

    dt = _softplus(sm_ref[...] + dtb_ref[...])
    dta = dt * (-jnp.exp(alog_ref[...]))
    expand = ex_ref[0]
    outs = []
    for c in range(rows // tb):
        r0, r1 = c * tb, (c + 1) * tb
        if carry:
            states = [h_ref[0].reshape(gw, D_STATE)]
        else:
            states = [h0_ref[i].reshape(gw, D_STATE) for i in range(len(segs))]
        y, new_states = _ssd_chunk(xc[r0:r1, 0:gw], xc[r0:r1, gw:gw + D_STATE], xc[r0:r1, gw + D_STATE:],
                                   dt[r0:r1], dta[r0:r1], expand, segs, states, eye)
        if carry:
            h_ref[0] = new_states[0].reshape(HEADS_PER_GROUP, SSD_HEAD_DIM, D_STATE)
        else:
            for i, s in enumerate(new_states):
                h_ref[i] = s.reshape(HEADS_PER_GROUP, SSD_HEAD_DIM, D_STATE)
        outs.append(y)
    y = outs[0] if len(outs) == 1 else jnp.concatenate(outs, axis=0)
    y = y + dsk_ref[...] * xc[:, 0:gw]
    y = y * _silu(z_ref[...])
    y = y * lax.rsqrt(jnp.mean(jnp.square(y), axis=-1, keepdims=True) + EPS)
    ys_ref[...] = (y * nw_ref[...]).astype(ys_ref.dtype)


def _ssd_scan(u, usm, conv_w, conv_b, dtb, alog, expand, dsk, nw, conv0, h0, *,
              n_seq, seq_rows, block_rows, tb, segs, carry):
    m = u.shape[0]
    kern = functools.partial(_ssd_kernel, tb=tb, segs=segs, carry=carry)
    gx, gb, gc, gz = U_X // GROUP_WIDTH, U_B // D_STATE, U_C // D_STATE, U_Z // GROUP_WIDTH
    cvb = SSD_WIDTH // D_STATE
    cvc = cvb + SSD_GROUPS
    if carry:
        nb = seq_rows // block_rows
        grid = (n_seq, SSD_GROUPS, nb)
        c_mul = 1 if conv0.shape[0] == n_seq else 0
        h_mul = 1 if h0.shape[0] == n_seq else 0
        row = lambda b, g, c: b * nb + c
        ublk = lambda width, off: pl.BlockSpec((block_rows, width), lambda b, g, c: (row(b, g, c), off + g))
        wblk = lambda nrow, width, off: pl.BlockSpec((nrow, width), lambda b, g, c: (0, off + g))
        const = lambda shape: pl.BlockSpec(shape, lambda b, g, c: (0,) * len(shape))
        csblk = lambda width, off: pl.BlockSpec((1, 8, width), lambda b, g, c: (b * c_mul, 0, off + g))
        hblk = lambda mul: pl.BlockSpec((1, HEADS_PER_GROUP, SSD_HEAD_DIM, D_STATE),
                                        lambda b, g, c: (b * mul, g, 0, 0))
        exblk = pl.BlockSpec((1, LANE, GROUP_WIDTH), lambda b, g, c: (g, 0, 0))
        smblk = pl.BlockSpec((block_rows, LANE), lambda b, g, c: (row(b, g, c), 0))
        scratch = [pltpu.VMEM((block_rows + 8, GROUP_WIDTH + 2 * D_STATE), F32)]
        sem = ("parallel", "parallel", "arbitrary")
        h0_spec, hout_spec = hblk(h_mul), hblk(1)
    else:
        per = len(segs)
        grid = (m // block_rows, SSD_GROUPS)
        ublk = lambda width, off: pl.BlockSpec((block_rows, width), lambda i, g: (i, off + g))
        wblk = lambda nrow, width, off: pl.BlockSpec((nrow, width), lambda i, g: (0, off + g))
        const = lambda shape: pl.BlockSpec(shape, lambda i, g: (0,) * len(shape))
        csblk = lambda width, off: pl.BlockSpec((per, CONV_W - 1, width), lambda i, g: (i, 0, off + g))
        h0_spec = hout_spec = pl.BlockSpec((per, HEADS_PER_GROUP, SSD_HEAD_DIM, D_STATE), lambda i, g: (i, g, 0, 0))
        exblk = pl.BlockSpec((1, LANE, GROUP_WIDTH), lambda i, g: (g, 0, 0))
        smblk = pl.BlockSpec((block_rows, LANE), lambda i, g: (i, 0))
        cw = GROUP_WIDTH + 2 * D_STATE
        scratch = [pltpu.VMEM((8 * per + 8, cw), F32), pltpu.VMEM((8 * per, cw), F32),
                   pltpu.VMEM((block_rows, cw), F32)]
        sem = ("parallel", "parallel")
    in_specs = [
        ublk(GROUP_WIDTH, gx), ublk(D_STATE, gb), ublk(D_STATE, gc), ublk(GROUP_WIDTH, gz), smblk,
        wblk(CONV_W, GROUP_WIDTH, 0), wblk(CONV_W, D_STATE, cvb), wblk(CONV_W, D_STATE, cvc),
        wblk(1, GROUP_WIDTH, 0), wblk(1, D_STATE, cvb), wblk(1, D_STATE, cvc),
        const((1, LANE)), const((1, LANE)), exblk,
        wblk(1, GROUP_WIDTH, 0), wblk(1, GROUP_WIDTH, 0),
        csblk(GROUP_WIDTH, 0), csblk(D_STATE, cvb), csblk(D_STATE, cvc),
        h0_spec,
    ]
    out_specs = [ublk(GROUP_WIDTH, 0), hout_spec]
    return pl.pallas_call(
        kern, grid=grid, in_specs=in_specs, out_specs=out_specs,
        out_shape=[jax.ShapeDtypeStruct((m, SSD_WIDTH), BF16),
                   jax.ShapeDtypeStruct((n_seq, SSD_HEADS, SSD_HEAD_DIM, D_STATE), F32)],
        scratch_shapes=scratch,
        compiler_params=pltpu.CompilerParams(dimension_semantics=sem, vmem_limit_bytes=VMEM_LIMIT),
        name="ssd_scan_carry" if carry else "ssd_scan_multi",
    )(u, u, u, u, usm, conv_w, conv_w, conv_w, conv_b, conv_b, conv_b, dtb, alog, expand, dsk, nw,
      conv0, conv0, conv0, h0)


def _matmul_kernel(x_ref, w_ref, o_ref, *, act):
    acc = _mm(x_ref[...], w_ref[...])
    if act == "relu2":
        acc = jnp.square(jnp.maximum(acc, 0.0))
    o_ref[...] = acc.astype(o_ref.dtype)


def _matmul(x, w, *, tm, tn, out_dtype, act=None, name):
    m, k = x.shape
    n = w.shape[1]
    return pl.pallas_call(
        functools.partial(_matmul_kernel, act=act),
        grid=(m // tm, n // tn),
        in_specs=[pl.BlockSpec((tm, k), lambda i, j: (i, 0)), pl.BlockSpec((k, tn), lambda i, j: (0, j))],
        out_specs=pl.BlockSpec((tm, tn), lambda i, j: (i, j)),
        out_shape=jax.ShapeDtypeStruct((m, n), out_dtype),
        compiler_params=pltpu.CompilerParams(dimension_semantics=("parallel", "parallel"),
                                             vmem_limit_bytes=VMEM_LIMIT),
        name=name,
    )(x, w)


def _out_proj_kernel(a_ref, b_ref, wa_ref, wb_ref, res_ref, o_ref):
    acc = _mm(a_ref[...], wa_ref[...]) + _mm(b_ref[...], wb_ref[...])
    o_ref[...] = ALPHA * res_ref[...] + acc


def _out_proj(a, b, w, res, *, tm, tn):
    m, kh = a.shape
    n = w.shape[1]
    return pl.pallas_call(
        _out_proj_kernel,
        grid=(m // tm, n // tn),
        in_specs=[pl.BlockSpec((tm, kh), lambda i, j: (i, 0)), pl.BlockSpec((tm, kh), lambda i, j: (i, 0)),
                  pl.BlockSpec((kh, tn), lambda i, j: (0, j)), pl.BlockSpec((kh, tn), lambda i, j: (1, j)),
                  pl.BlockSpec((tm, tn), lambda i, j: (i, j))],
        out_specs=pl.BlockSpec((tm, tn), lambda i, j: (i, j)),
        out_shape=jax.ShapeDtypeStruct((m, n), F32),
        compiler_params=pltpu.CompilerParams(dimension_semantics=("parallel", "parallel"),
                                             vmem_limit_bytes=VMEM_LIMIT),
        name="out_proj",
    )(a, b, w, w, res)


def _down_proj_kernel(x_ref, w_ref, res_ref, o_ref, acc_ref):
    kk = pl.program_id(2)

    @pl.when(kk == 0)
    def _():
        acc_ref[...] = ALPHA * res_ref[...]

    acc_ref[...] += _mm(x_ref[...], w_ref[...])

    @pl.when(kk == pl.num_programs(2) - 1)
    def _():
        o_ref[...] = acc_ref[...]


def _down_proj(x, w, res, *, tm, tn, tk):
    m, k = x.shape
    n = w.shape[1]
    return pl.pallas_call(
        _down_proj_kernel,
        grid=(m // tm, n // tn, k // tk),
        in_specs=[pl.BlockSpec((tm, tk), lambda i, j, kk: (i, kk)), pl.BlockSpec((tk, tn), lambda i, j, kk: (kk, j)),
                  pl.BlockSpec((tm, tn), lambda i, j, kk: (i, j))],
        out_specs=pl.BlockSpec((tm, tn), lambda i, j, kk: (i, j)),
        out_shape=jax.ShapeDtypeStruct((m, n), F32),
        scratch_shapes=[pltpu.VMEM((tm, tn), F32)],
        compiler_params=pltpu.CompilerParams(dimension_semantics=("parallel", "parallel", "arbitrary"),
                                             vmem_limit_bytes=VMEM_LIMIT),
        name="down_proj",
    )(x, w, res)


def _layer_norm_kernel(x_ref, g_ref, b_ref, o_ref, ob_ref):
    x = x_ref[...]
    mu = jnp.mean(x, axis=-1, keepdims=True)
    xc = x - mu
    var = jnp.mean(jnp.square(xc), axis=-1, keepdims=True)
    y = xc * lax.rsqrt(var + EPS) * g_ref[...] + b_ref[...]
    o_ref[...] = y
    ob_ref[...] = y.astype(BF16)


def _layer_norm(x, g, b, *, tm):
    m, d = x.shape
    return pl.pallas_call(
        _layer_norm_kernel,
        grid=(m // tm,),
        in_specs=[pl.BlockSpec((tm, d), lambda i: (i, 0)), pl.BlockSpec((1, d), lambda i: (0, 0)),
                  pl.BlockSpec((1, d), lambda i: (0, 0))],
        out_specs=[pl.BlockSpec((tm, d), lambda i: (i, 0)), pl.BlockSpec((tm, d), lambda i: (i, 0))],
        out_shape=[jax.ShapeDtypeStruct((m, d), F32), jax.ShapeDtypeStruct((m, d), BF16)],
        compiler_params=pltpu.CompilerParams(dimension_semantics=("parallel",), vmem_limit_bytes=VMEM_LIMIT),
        name="layer_norm",
    )(x, g, b)


def _expand_matrix():
    e = np.zeros((SSD_GROUPS, LANE, GROUP_WIDTH), np.float32)
    for g in range(SSD_GROUPS):
        for j in range(HEADS_PER_GROUP):
            e[g, SM_DT + g * HEADS_PER_GROUP + j, j * SSD_HEAD_DIM:(j + 1) * SSD_HEAD_DIM] = 1.0
    return jnp.asarray(e, BF16)


def _lane_param(p):
    return jnp.zeros((1, LANE), F32).at[0, SM_DT:SM_DT + SSD_HEADS].set(p.astype(F32))


def _dense_tail(x, og, ys, prm, tm):
    pre1 = _out_proj(og, ys, prm["w_out"], x, tm=tm, tn=1024)
    h, hb = _layer_norm(pre1, prm["ln1_g"], prm["ln1_b"], tm=min(tm, 256))
    hid = _matmul(hb, prm["w_up"], tm=tm, tn=1024, out_dtype=BF16, act="relu2", name="mlp_up")
    pre2 = _down_proj(hid, prm["w_down"], h, tm=tm, tn=1024, tk=2048)
    out, _ = _layer_norm(pre2, prm["ln2_g"], prm["ln2_b"], tm=min(tm, 256))
    return out


def kernel(x_prompt, x_sample, state_gla, state_ssm, state_conv, meta_tokens, w_in, w_gk_up, b_gk, gla_norm_w,
           conv_w, conv_b, dt_bias, a_log, d_skip, ssd_norm_w, w_out, ln1_g, ln1_b, w_up, w_down, ln2_g, ln2_b):
    bp, seq, d = x_prompt.shape
    bs, dec = x_sample.shape[:2]
    assert w_in.shape[0] == DEPTH and seq % CHUNK == 0 and dec == 4 and N_META <= CHUNK
    wi = w_in[0]
    o_g = U_Z
    o_z = o_g + GATE_RANK
    o_dt = o_z + SSD_WIDTH + CONV_DIM
    w_main = jnp.concatenate([wi[:, :o_g], wi[:, o_z:o_dt]], axis=1).astype(BF16)
    w_small = jnp.concatenate([wi[:, o_g:o_z], wi[:, o_dt:], jnp.zeros((d, LANE - GATE_RANK - SSD_HEADS), F32)],
                              axis=1).astype(BF16)
    wg = jnp.concatenate([w_gk_up[0], jnp.zeros((LANE - GATE_RANK, GLA_QK), F32)], axis=0).astype(BF16)
    prm = dict(
        w_out=w_out[0].astype(BF16), w_up=w_up[0].astype(BF16), w_down=w_down[0].astype(BF16),
        ln1_g=ln1_g, ln1_b=ln1_b, ln2_g=ln2_g, ln2_b=ln2_b,
    )
    gla_args = (wg, b_gk, gla_norm_w)
    ssd_args = (conv_w[0], conv_b, _lane_param(dt_bias[0]), _lane_param(a_log[0]), _expand_matrix(),
                jnp.repeat(d_skip[0], SSD_HEAD_DIM)[None, :], ssd_norm_w)

    def in_proj(xb, tm):
        u = _matmul(xb, w_main, tm=tm, tn=1024, out_dtype=F32, name="in_proj")
        usm = _matmul(xb, w_small, tm=tm, tn=LANE, out_dtype=F32, name="in_proj_small")
        return u, usm

    xm = jnp.concatenate([jnp.zeros((CHUNK - N_META, d), F32), meta_tokens], axis=0).astype(BF16)
    um, usm_m = in_proj(xm, CHUNK)
    meta_seg = ((CHUNK - N_META, CHUNK),)
    _, gla_m = _gla_scan(um, usm_m, *gla_args, jnp.zeros((1, GLA_HEADS, GLA_DK, GLA_DV), F32),
                         n_seq=1, seq_rows=CHUNK, block_rows=CHUNK, tb=CHUNK, segs=meta_seg, carry=True)
    _, ssm_m = _ssd_scan(um, usm_m, *ssd_args, jnp.zeros((1, 8, CONV_DIM), F32),
                         jnp.zeros((1, SSD_HEADS, SSD_HEAD_DIM, D_STATE), F32),
                         n_seq=1, seq_rows=CHUNK, block_rows=CHUNK, tb=CHUNK, segs=meta_seg, carry=True)
    conv_m = um[None, CHUNK - 8:, U_X:]

    xp = x_prompt.reshape(bp * seq, d)
    up, usm_p = in_proj(xp.astype(BF16), 1024)
    full_seg = ((0, CHUNK),)
    og_p, gla_p = _gla_scan(up, usm_p, *gla_args, gla_m, n_seq=bp, seq_rows=seq, block_rows=256, tb=CHUNK,
                            segs=full_seg, carry=True)
    ys_p, ssm_p = _ssd_scan(up, usm_p, *ssd_args, conv_m, ssm_m, n_seq=bp, seq_rows=seq, block_rows=256,
                            tb=CHUNK, segs=full_seg, carry=True)
    y_prompt = _dense_tail(xp, og_p, ys_p, prm, 1024).reshape(bp, seq, d)
    conv_p = up.reshape(bp, seq, U_MAIN)[:, seq - (CONV_W - 1):, U_X:]

    xs = x_sample.reshape(bs * dec, d)
    us, usm_s = in_proj(xs.astype(BF16), bs * dec)
    seg4 = tuple((dec * i, dec * (i + 1)) for i in range(4))
    og_s, gla_s = _gla_scan(us, usm_s, *gla_args, state_gla[0], n_seq=bs, seq_rows=dec, block_rows=4 * dec,
                            tb=4 * dec, segs=seg4, carry=False)
    ys_s, ssm_s = _ssd_scan(us, usm_s, *ssd_args, state_conv[0], state_ssm[0], n_seq=bs, seq_rows=dec,
                            block_rows=4 * dec, tb=4 * dec, segs=seg4, carry=False)
    y_sample = _dense_tail(xs, og_s, ys_s, prm, bs * dec).reshape(bs, dec, d)
    conv_s = us.reshape(bs, dec, U_MAIN)[:, dec - (CONV_W - 1):, U_X:]

    return (y_prompt, y_sample, gla_p[None], ssm_p[None], conv_p[None], gla_s[None], ssm_s[None], conv_s[None])
```

```python
import functools

import numpy as np
import jax
import jax.numpy as jnp
from jax import lax
from jax.experimental import pallas as pl
from jax.experimental.pallas import tpu as pltpu

F32 = jnp.float32
BF16 = jnp.bfloat16

GLA_HEADS = 4
GLA_DK = 256
GLA_DV = 512
GLA_QK = GLA_HEADS * GLA_DK
GLA_WIDTH = GLA_HEADS * GLA_DV
GATE_RANK = 16
GATE_NORM = 16.0
SSD_GROUPS = 8
HEADS_PER_GROUP = 4
SSD_HEAD_DIM = 64
SSD_HEADS = SSD_GROUPS * HEADS_PER_GROUP
GROUP_WIDTH = HEADS_PER_GROUP * SSD_HEAD_DIM
SSD_WIDTH = SSD_GROUPS * GROUP_WIDTH
D_STATE = 128
BC_WIDTH = SSD_GROUPS * D_STATE
CONV_W = 4
CONV_DIM = SSD_WIDTH + 2 * BC_WIDTH
N_META = 16
CHUNK = 64
CHUNK_SHIFT = 6
EPS = 1e-5
DEPTH = 1
ALPHA = (2.0 * DEPTH) ** 0.25

UA_K = GLA_QK
UA_V = 2 * GLA_QK
UA_R = UA_V + GLA_WIDTH
UA_WIDTH = UA_R + GLA_WIDTH
UB_X = SSD_WIDTH
UB_B = UB_X + SSD_WIDTH
UB_C = UB_B + BC_WIDTH
UB_WIDTH = UB_C + BC_WIDTH
LANE = 128
SM_DT = GATE_RANK

GLA_HEADS_PER_STEP = 2
SSD_GROUPS_PER_STEP = 2
SEQS_PER_BLOCK = 4

VMEM_LIMIT = 56 * 1024 * 1024


def _mm(a, b):
    return jnp.dot(a, b, preferred_element_type=F32)


def _nt(a, b):
    return lax.dot_general(a, b, (((1,), (1,)), ((), ())), preferred_element_type=F32)


def _split2(x):
    hi = x.astype(BF16)
    lo = (x - hi.astype(F32)).astype(BF16)
    return hi, lo


def _mm_split(m01, x):
    hi, lo = _split2(x)
    return _mm(m01, hi) + _mm(m01, lo)


def _pad_rows(x, n):
    t = x.shape[0]
    if t >= n:
        return x
    return jnp.concatenate([x, jnp.zeros((n - t, x.shape[1]), x.dtype)], axis=0)


def _silu(x):
    return x * (1.0 / (1.0 + jnp.exp(-x)))


def _softplus(x):
    return jnp.maximum(x, 0.0) + jnp.log(1.0 + jnp.exp(-jnp.abs(x)))


def _f01(m):
    return jnp.where(m, 1.0, 0.0).astype(BF16)


def _eye_bf16(n):
    ri = lax.broadcasted_iota(jnp.int32, (n, n), 0)
    ci = lax.broadcasted_iota(jnp.int32, (n, n), 1)
    return _f01(ri == ci)


def _split3_rows(x, period):
    hi = x.astype(BF16)
    r1 = x - hi.astype(F32)
    mid = r1.astype(BF16)
    lo = (r1 - mid.astype(F32)).astype(BF16)
    ridx = lax.broadcasted_iota(jnp.int32, x.shape, 0) & (period - 1)
    out = jnp.where(ridx == 0, hi.astype(F32),
                    jnp.where(ridx == 1, mid.astype(F32),
                              jnp.where(ridx == 2, lo.astype(F32), 0.0)))
    return out.astype(BF16)


def _lane_group_sum(t, shift, idx):
    lane = lax.broadcasted_iota(jnp.int32, t.shape, 1)
    return jnp.sum(jnp.where((lane >> shift) == idx, t, 0.0), axis=1, keepdims=True)


def _lane_group_keep(t, shift, idx):
    lane = lax.broadcasted_iota(jnp.int32, t.shape, 1)
    return jnp.where((lane >> shift) == idx, t, 0.0).astype(BF16)


def _chunk_masks(rows, cols, lead_null, lane_mod=None):
    ri = lax.broadcasted_iota(jnp.int32, (rows, cols), 0)
    ci = lax.broadcasted_iota(jnp.int32, (rows, cols), 1)
    if lane_mod is None:
        same = (ri >> CHUNK_SHIFT) == (ci >> CHUNK_SHIFT)
        cl = ci & (CHUNK - 1)
    else:
        cl = ci & lane_mod
        same = None
    rl = ri & (CHUNK - 1)
    valid = cl >= lead_null if lead_null else None
    base = same
    if valid is not None:
        base = valid if base is None else (base & valid)
    causal = (cl <= rl) if base is None else (base & (cl <= rl))
    middle = (lead_null + CHUNK) // 2 - 1
    upto_mid = (cl <= middle) if base is None else (base & (cl <= middle))
    return base, causal, upto_mid


def _seg_masks(tb, cols, seg_len, lane_mod=None):
    ri = lax.broadcasted_iota(jnp.int32, (tb, cols), 0)
    ci = lax.broadcasted_iota(jnp.int32, (tb, cols), 1)
    if lane_mod is not None:
        ci = ci & lane_mod
    shift = seg_len.bit_length() - 1
    same = ((ri >> shift) == (ci >> shift)) & (ci < tb)
    causal = same & (ci <= ri)
    upto_mid = same & ((ci & (seg_len - 1)) <= seg_len // 2 - 1)
    return same, causal, upto_mid


def _gla_gate(sm_b, wg, bg):
    gpre = _mm(sm_b, wg) + bg
    return -_softplus(-gpre) * (1.0 / GATE_NORM)


def _gla_finish(o, gnw, r):
    o = o * lax.rsqrt(jnp.mean(jnp.square(o), axis=-1, keepdims=True) + EPS)
    return o * gnw * _silu(r)


def _gla_block(q, k, v, g, mats, causal_p, eye):
    tri, midm, allm = mats
    rows, dk = q.shape
    rows_p = max(rows, LANE)
    bc = _mm_split(tri, g)
    anchor = _mm_split(midm, g)
    blast = _mm_split(allm, g)
    qs = q * (dk ** -0.5)
    qt = (qs * jnp.exp(bc - anchor)).astype(BF16)
    kt = _pad_rows((k * jnp.exp(anchor - bc)).astype(BF16), rows_p)
    p = jnp.where(causal_p, _nt(qt, kt), 0.0)
    vb = _pad_rows(v.astype(BF16), rows_p)
    o = _mm(p.astype(BF16), vb)
    qb = (qs * jnp.exp(bc)).astype(BF16)
    kd = _pad_rows((k * jnp.exp(blast - bc)).astype(BF16), rows_p)
    kd_t = _nt(eye, kd)
    return o, qb, kd_t, vb, jnp.exp(blast)


def _gla_carry_kernel(q_ref, k_ref, v_ref, r_ref, sm_ref, wg_ref, bg_ref, gnw_ref, s0_ref,
                      og_ref, s_ref, *, lead_null):
    rows = q_ref.shape[0]
    nchunk = rows // CHUNK
    rows_p = max(rows, LANE)

    @pl.when(pl.program_id(2) == 0)
    def _():
        s_ref[...] = s0_ref[...]

    eye = _eye_bf16(GLA_DK)
    same, causal, upto_mid = _chunk_masks(rows, rows, lead_null)
    mats = (_f01(causal), _f01(upto_mid), _f01(same))
    causal_p = _chunk_masks(rows, rows_p, lead_null)[1]
    sm_b = sm_ref[...].astype(BF16)
    for hh in range(GLA_HEADS_PER_STEP):
        ks = slice(hh * GLA_DK, (hh + 1) * GLA_DK)
        vs = slice(hh * GLA_DV, (hh + 1) * GLA_DV)
        g = _gla_gate(sm_b, wg_ref[:, ks], bg_ref[:, ks])
        k = k_ref[:, ks]
        if lead_null:
            rl = lax.broadcasted_iota(jnp.int32, (rows, 1), 0) & (CHUNK - 1)
            k = jnp.where(rl >= lead_null, k, 0.0)
        o, qb, kd_t, vb, e_last = _gla_block(q_ref[:, ks], k, v_ref[:, vs], g, mats, causal_p, eye)
        e_rows = [_split3_rows(e_last[c * CHUNK:c * CHUNK + 16], 16) for c in range(nchunk)]
        e_rows.append(jnp.zeros((LANE - 16 * nchunk, GLA_DK), BF16))
        e_t = _nt(eye, jnp.concatenate(e_rows, axis=0))
        s = s_ref[0, hh]
        inter = []
        for c in range(nchunk):
            inter.append(_mm(qb[c * CHUNK:(c + 1) * CHUNK], s.astype(BF16)))
            s = _lane_group_sum(e_t, 4, c) * s + _mm(_lane_group_keep(kd_t, CHUNK_SHIFT, c), vb)
        s_ref[0, hh] = s
        o = o + (inter[0] if nchunk == 1 else jnp.concatenate(inter, axis=0))
        og_ref[:, vs] = _gla_finish(o, gnw_ref[...], r_ref[:, vs]).astype(og_ref.dtype)


def _gla_multi_kernel(q_ref, k_ref, v_ref, r_ref, sm_ref, wg_ref, bg_ref, gnw_ref, s0_ref,
                      og_ref, s_ref, *, seg_len):
    tb = q_ref.shape[0]
    nseq = tb // seg_len
    seg_shift = seg_len.bit_length() - 1
    eye = _eye_bf16(GLA_DK)
    same, causal, upto_mid = _seg_masks(tb, tb, seg_len)
    mats = (_f01(causal), _f01(upto_mid), _f01(same))
    causal_p = _seg_masks(tb, LANE, seg_len)[1]
    sm_b = sm_ref[...].astype(BF16)
    for hh in range(GLA_HEADS_PER_STEP):
        ks = slice(hh * GLA_DK, (hh + 1) * GLA_DK)
        vs = slice(hh * GLA_DV, (hh + 1) * GLA_DV)
        g = _gla_gate(sm_b, wg_ref[:, ks], bg_ref[:, ks])
        o, qb, kd_t, vb, e_last = _gla_block(q_ref[:, ks], k_ref[:, ks], v_ref[:, vs], g, mats, causal_p, eye)
        e_rows = jnp.concatenate([_split3_rows(e_last, seg_len), jnp.zeros((LANE - tb, GLA_DK), BF16)], axis=0)
        e_t = _nt(eye, e_rows)
        rowseg = lax.broadcasted_iota(jnp.int32, (tb, 1), 0) >> seg_shift
        for i in range(nseq):
            s = s0_ref[i, hh]
            o = o + jnp.where(rowseg == i, _mm(qb, s.astype(BF16)), 0.0)
            s_ref[i, hh] = (_lane_group_sum(e_t, seg_shift, i) * s
                            + _mm(_lane_group_keep(kd_t, seg_shift, i), vb))
        og_ref[:, vs] = _gla_finish(o, gnw_ref[...], r_ref[:, vs]).astype(og_ref.dtype)


def _gla_scan(ua, usm, wg, bg, gnw, s0, *, n_seq, seq_rows, block_rows, carry, lead_null=0):
    m = ua.shape[0]
    nh = GLA_HEADS_PER_STEP
    kw, vw = nh * GLA_DK, nh * GLA_DV
    if carry:
        nb = seq_rows // block_rows
        grid = (n_seq, GLA_HEADS // nh, nb)
        s0_mul = 1 if s0.shape[0] == n_seq else 0
        rmap = lambda off: (lambda b, h, c: (b * nb + c, off + h))
        smap = lambda b, h, c: (b * nb + c, 0)
        cmap = lambda b, h, c: (0, h)
        zmap = lambda b, h, c: (0, 0)
        s0_spec = pl.BlockSpec((1, nh, GLA_DK, GLA_DV), lambda b, h, c: (b * s0_mul, h, 0, 0))
        so_spec = pl.BlockSpec((1, nh, GLA_DK, GLA_DV), lambda b, h, c: (b, h, 0, 0))
        sem = ("parallel", "parallel", "arbitrary")
        kern = functools.partial(_gla_carry_kernel, lead_null=lead_null)
    else:
        grid = (m // block_rows, GLA_HEADS // nh)
        rmap = lambda off: (lambda i, h: (i, off + h))
        smap = lambda i, h: (i, 0)
        cmap = lambda i, h: (0, h)
        zmap = lambda i, h: (0, 0)
        s0_spec = so_spec = pl.BlockSpec((SEQS_PER_BLOCK, nh, GLA_DK, GLA_DV), lambda i, h: (i, h, 0, 0))
        sem = ("parallel", "parallel")
        kern = functools.partial(_gla_multi_kernel, seg_len=seq_rows)
    in_specs = [
        pl.BlockSpec((block_rows, kw), rmap(0)), pl.BlockSpec((block_rows, kw), rmap(UA_K // kw)),
        pl.BlockSpec((block_rows, vw), rmap(UA_V // vw)), pl.BlockSpec((block_rows, vw), rmap(UA_R // vw)),
        pl.BlockSpec((block_rows, LANE), smap),
        pl.BlockSpec((LANE, kw), cmap), pl.BlockSpec((1, kw), cmap), pl.BlockSpec((1, GLA_DV), zmap),
        s0_spec,
    ]
    out_specs = [pl.BlockSpec((block_rows, vw), rmap(0)), so_spec]
    return pl.pallas_call(
        kern, grid=grid, in_specs=in_specs, out_specs=out_specs,
        out_shape=[jax.ShapeDtypeStruct((m, GLA_WIDTH), BF16),
                   jax.ShapeDtypeStruct((n_seq, GLA_HEADS, GLA_DK, GLA_DV), F32)],
        compiler_params=pltpu.CompilerParams(dimension_semantics=sem, vmem_limit_bytes=VMEM_LIMIT),
        name="gla_scan_carry" if carry else "gla_scan_multi",
    )(ua, ua, ua, ua, usm, wg, bg, gnw, s0)


def _ssd_block(x, dt, dta, expand, tri, allm, causal_l, eye, row_valid):
    rows, gw = x.shape
    rows_p = max(rows, LANE)
    p_hi, p_lo = [_mm(p, expand).astype(BF16) for p in _split2(dta)]
    cb = _mm(tri, p_hi) + _mm(tri, p_lo)
    cl = _mm(allm, p_hi) + _mm(allm, p_lo)
    ri = lax.broadcasted_iota(jnp.int32, (rows, gw), 0)
    li = lax.broadcasted_iota(jnp.int32, (rows, gw), 1)
    diag = jnp.where((li & (CHUNK - 1)) == (ri & (CHUNK - 1)), cb, 0.0)
    rw = _mm_split(allm, diag)
    decay = jnp.exp(jnp.where(causal_l, cb - rw, -jnp.inf))
    dt_hi, dt_lo = _split2(dt)
    xdt = x * (_mm(dt_hi, expand) + _mm(dt_lo, expand))
    if row_valid is not None:
        xdt = jnp.where(row_valid, xdt, 0.0)
    xw_t = _nt(eye, _pad_rows((xdt * jnp.exp(cl - cb)).astype(BF16), rows_p))
    return decay, xdt, jnp.exp(cb), xw_t, jnp.exp(cl)


def _ssd_intra(cm_b, bm_b, decay, xdt, seg_rows):
    gw = xdt.shape[1]
    b64 = _pad_rows(bm_b, CHUNK)
    cb4 = _nt(cm_b, jnp.concatenate([b64] * HEADS_PER_GROUP, axis=0))
    x64 = _pad_rows(xdt, CHUNK)
    r2 = lax.broadcasted_iota(jnp.int32, (gw, gw), 0)
    l2 = lax.broadcasted_iota(jnp.int32, (gw, gw), 1)
    bd = jnp.where((r2 >> CHUNK_SHIFT) == (l2 >> CHUNK_SHIFT),
                   jnp.concatenate([x64] * HEADS_PER_GROUP, axis=0), 0.0)
    return _mm((cb4 * decay).astype(BF16), bd.astype(BF16))


def _conv_rows(xpad, cw_refs, cb_refs, n):
    acc = jnp.concatenate([r[...] for r in cb_refs], axis=1)
    acc = jnp.broadcast_to(acc, (n, acc.shape[1]))
    for d in range(CONV_W):
        wrow = jnp.concatenate([w[pl.ds(CONV_W - 1 - d, 1), :] for w in cw_refs], axis=1)
        acc = acc + wrow * xpad[pl.ds(8 - d, n), :]
    return acc


def _ssd_finish(y, x, dsk, z, nw):
    y = y + dsk * x
    y = y * _silu(z)
    y = y * lax.rsqrt(jnp.mean(jnp.square(y), axis=-1, keepdims=True) + EPS)
    return y * nw


def _ssd_lanes():
    ng = SSD_GROUPS_PER_STEP
    xw, sw = ng * GROUP_WIDTH, ng * D_STATE
    return ((0, xw), (xw, xw + sw), (xw + sw, xw + 2 * sw))


def _ssd_carry_kernel(x_ref, b_ref, c_ref, z_ref, sm_ref, cwx_ref, cwb_ref, cwc_ref, cbx_ref, cbb_ref, cbc_ref,
                      dtb_ref, alog_ref, ex_ref, dsk_ref, nw_ref, csx_ref, csb_ref, csc_ref, h0_ref,
                      ys_ref, h_ref, xpad, *, lead_null):
    rows = x_ref.shape[0]
    nchunk = rows // CHUNK
    rows_p = max(rows, LANE)
    gw = GROUP_WIDTH
    lanes = _ssd_lanes()

    @pl.when(pl.program_id(2) == 0)
    def _():
        h_ref[...] = h0_ref[...]
        for (lo, hi), cs in zip(lanes, (csx_ref, csb_ref, csc_ref)):
            xpad[pl.ds(0, 8), lo:hi] = cs[0]

    for (lo, hi), src in zip(lanes, (x_ref, b_ref, c_ref)):
        xpad[pl.ds(8, rows), lo:hi] = src[...]
    xc = _silu(_conv_rows(xpad, (cwx_ref, cwb_ref, cwc_ref), (cbx_ref, cbb_ref, cbc_ref), rows))
    xpad[pl.ds(0, 8), :] = xpad[pl.ds(rows, 8), :]

    eye = _eye_bf16(gw)
    same, causal, _ = _chunk_masks(rows, rows, lead_null)
    tri, allm = _f01(causal), _f01(same)
    causal_l = _chunk_masks(rows, gw, lead_null, lane_mod=CHUNK - 1)[1]
    row_valid = None
    if lead_null:
        row_valid = (lax.broadcasted_iota(jnp.int32, (rows, 1), 0) & (CHUNK - 1)) >= lead_null
    dt = _softplus(sm_ref[...] + dtb_ref[...])
    dta = dt * (-jnp.exp(alog_ref[...]))
    for gg in range(SSD_GROUPS_PER_STEP):
        xs = slice(gg * gw, (gg + 1) * gw)
        x = xc[:, lanes[0][0] + gg * gw:lanes[0][0] + (gg + 1) * gw]
        bm_b = xc[:, lanes[1][0] + gg * D_STATE:lanes[1][0] + (gg + 1) * D_STATE].astype(BF16)
        cm_b = xc[:, lanes[2][0] + gg * D_STATE:lanes[2][0] + (gg + 1) * D_STATE].astype(BF16)
        decay, xdt, ecb, xw_t, e_last = _ssd_block(x, dt, dta, ex_ref[gg], tri, allm, causal_l, eye, row_valid)
        bpad = _pad_rows(bm_b, rows_p)
        e_rows = [_split3_rows(e_last[c * CHUNK:c * CHUNK + 16], 16) for c in range(nchunk)]
        e_rows.append(jnp.zeros((LANE - 16 * nchunk, gw), BF16))
        e_t = _nt(eye, jnp.concatenate(e_rows, axis=0))
        hs = slice(gg * HEADS_PER_GROUP, (gg + 1) * HEADS_PER_GROUP)
        h = h_ref[0, hs].reshape(gw, D_STATE)
        ys = []
        for c in range(nchunk):
            rs = slice(c * CHUNK, (c + 1) * CHUNK)
            y = _ssd_intra(cm_b[rs], bm_b[rs], decay[rs], xdt[rs], CHUNK)
            ys.append(y + _nt(cm_b[rs], h.astype(BF16)) * ecb[rs])
            h = _lane_group_sum(e_t, 4, c) * h + _mm(_lane_group_keep(xw_t, CHUNK_SHIFT, c), bpad)
        h_ref[0, hs] = h.reshape(HEADS_PER_GROUP, SSD_HEAD_DIM, D_STATE)
        y = ys[0] if nchunk == 1 else jnp.concatenate(ys, axis=0)
        ys_ref[:, xs] = _ssd_finish(y, x, dsk_ref[:, xs], z_ref[:, xs], nw_ref[:, xs]).astype(ys_ref.dtype)


def _ssd_multi_kernel(x_ref, b_ref, c_ref, z_ref, sm_ref, cwx_ref, cwb_ref, cwc_ref, cbx_ref, cbb_ref, cbc_ref,
                      dtb_ref, alog_ref, ex_ref, dsk_ref, nw_ref, csx_ref, csb_ref, csc_ref, h0_ref,
                      ys_ref, h_ref, xpad, cres, xcs, *, seg_len):
    tb = x_ref.shape[0]
    nseq = tb // seg_len
    seg_shift = seg_len.bit_length() - 1
    gw = GROUP_WIDTH
    lanes = _ssd_lanes()
    xpad[...] = jnp.zeros(xpad.shape, F32)
    for i in range(nseq):
        for (lo, hi), src, cs in zip(lanes, (x_ref, b_ref, c_ref), (csx_ref, csb_ref, csc_ref)):
            xpad[pl.ds(8 + 8 * i + 1, CONV_W - 1), lo:hi] = cs[i]
            xpad[pl.ds(8 + 8 * i + 4, seg_len), lo:hi] = src[pl.ds(seg_len * i, seg_len), :]
    cres[...] = _conv_rows(xpad, (cwx_ref, cwb_ref, cwc_ref), (cbx_ref, cbb_ref, cbc_ref), 8 * nseq)
    for i in range(nseq):
        xcs[pl.ds(seg_len * i, seg_len), :] = cres[pl.ds(8 * i + 4, seg_len), :]
    xc = _silu(xcs[...])

    eye = _eye_bf16(gw)
    same, causal, _ = _seg_masks(tb, tb, seg_len)
    tri, allm = _f01(causal), _f01(same)
    causal_l = _seg_masks(tb, gw, seg_len, lane_mod=CHUNK - 1)[1]
    dt = _softplus(sm_ref[...] + dtb_ref[...])
    dta = dt * (-jnp.exp(alog_ref[...]))
    rowseg = lax.broadcasted_iota(jnp.int32, (tb, 1), 0) >> seg_shift
    for gg in range(SSD_GROUPS_PER_STEP):
        xs = slice(gg * gw, (gg + 1) * gw)
        x = xc[:, lanes[0][0] + gg * gw:lanes[0][0] + (gg + 1) * gw]
        bm_b = xc[:, lanes[1][0] + gg * D_STATE:lanes[1][0] + (gg + 1) * D_STATE].astype(BF16)
        cm_b = xc[:, lanes[2][0] + gg * D_STATE:lanes[2][0] + (gg + 1) * D_STATE].astype(BF16)
        decay, xdt, ecb, xw_t, e_last = _ssd_block(x, dt, dta, ex_ref[gg], tri, allm, causal_l, eye, None)
        bpad = _pad_rows(bm_b, LANE)
        e_rows = jnp.concatenate([_split3_rows(e_last, seg_len), jnp.zeros((LANE - tb, gw), BF16)], axis=0)
        e_t = _nt(eye, e_rows)
        y = _ssd_intra(cm_b, bm_b, decay, xdt, tb)
        hs = slice(gg * HEADS_PER_GROUP, (gg + 1) * HEADS_PER_GROUP)
        for i in range(nseq):
            h = h0_ref[i, hs].reshape(gw, D_STATE)
            y = y + jnp.where(rowseg == i, _nt(cm_b, h.astype(BF16)) * ecb, 0.0)
            h = _lane_group_sum(e_t, seg_shift, i) * h + _mm(_lane_group_keep(xw_t, seg_shift, i), bpad)
            h_ref[i, hs] = h.reshape(HEADS_PER_GROUP, SSD_HEAD_DIM, D_STATE)
        ys_ref[:, xs] = _ssd_finish(y, x, dsk_ref[:, xs], z_ref[:, xs], nw_ref[:, xs]).astype(ys_ref.dtype)


def _ssd_scan(ub, usm, conv_w, conv_b, dtb, alog, expand, dsk, nw, conv0, h0, *,
              n_seq, seq_rows, block_rows, carry, lead_null=0):
    m = ub.shape[0]
    ng = SSD_GROUPS_PER_STEP
    xw, sw = ng * GROUP_WIDTH, ng * D_STATE
    cw = xw + 2 * sw
    hpg = ng * HEADS_PER_GROUP
    if carry:
        nb = seq_rows // block_rows
        grid = (n_seq, SSD_GROUPS // ng, nb)
        c_mul = 1 if conv0.shape[0] == n_seq else 0
        h_mul = 1 if h0.shape[0] == n_seq else 0
        rmap = lambda off: (lambda b, g, c: (b * nb + c, off + g))
        smap = lambda b, g, c: (b * nb + c, 0)
        wmap = lambda off: (lambda b, g, c: (0, off + g))
        zmap = lambda b, g, c: (0, 0)
        exmap = lambda b, g, c: (g, 0, 0)
        cs_spec = lambda width, off: pl.BlockSpec((1, 8, width), lambda b, g, c: (b * c_mul, 0, off + g))
        h0_spec = pl.BlockSpec((1, hpg, SSD_HEAD_DIM, D_STATE), lambda b, g, c: (b * h_mul, g, 0, 0))
        ho_spec = pl.BlockSpec((1, hpg, SSD_HEAD_DIM, D_STATE), lambda b, g, c: (b, g, 0, 0))
        scratch = [pltpu.VMEM((block_rows + 8, cw), F32)]
        sem = ("parallel", "parallel", "arbitrary")
        kern = functools.partial(_ssd_carry_kernel, lead_null=lead_null)
    else:
        per = SEQS_PER_BLOCK
        grid = (m // block_rows, SSD_GROUPS // ng)
        rmap = lambda off: (lambda i, g: (i, off + g))
        smap = lambda i, g: (i, 0)
        wmap = lambda off: (lambda i, g: (0, off + g))
        zmap = lambda i, g: (0, 0)
        exmap = lambda i, g: (g, 0, 0)
        cs_spec = lambda width, off: pl.BlockSpec((per, CONV_W - 1, width), lambda i, g: (i, 0, off + g))
        h0_spec = ho_spec = pl.BlockSpec((per, hpg, SSD_HEAD_DIM, D_STATE), lambda i, g: (i, g, 0, 0))
        scratch = [pltpu.VMEM((8 * per + 8, cw), F32), pltpu.VMEM((8 * per, cw), F32),
                   pltpu.VMEM((block_rows, cw), F32)]
        sem = ("parallel", "parallel")
        kern = functools.partial(_ssd_multi_kernel, seg_len=seq_rows)
    cvb, cvc = SSD_WIDTH // sw, (SSD_WIDTH + BC_WIDTH) // sw
    in_specs = [
        pl.BlockSpec((block_rows, xw), rmap(UB_X // xw)), pl.BlockSpec((block_rows, sw), rmap(UB_B // sw)),
        pl.BlockSpec((block_rows, sw), rmap(UB_C // sw)), pl.BlockSpec((block_rows, xw), rmap(0)),
        pl.BlockSpec((block_rows, LANE), smap),
        pl.BlockSpec((CONV_W, xw), wmap(0)), pl.BlockSpec((CONV_W, sw), wmap(cvb)), pl.BlockSpec((CONV_W, sw), wmap(cvc)),
        pl.BlockSpec((1, xw), wmap(0)), pl.BlockSpec((1, sw), wmap(cvb)), pl.BlockSpec((1, sw), wmap(cvc)),
        pl.BlockSpec((1, LANE), zmap), pl.BlockSpec((1, LANE), zmap),
        pl.BlockSpec((ng, LANE, GROUP_WIDTH), exmap),
        pl.BlockSpec((1, xw), wmap(0)), pl.BlockSpec((1, xw), wmap(0)),
        cs_spec(xw, 0), cs_spec(sw, cvb), cs_spec(sw, cvc),
        h0_spec,
    ]
    out_specs = [pl.BlockSpec((block_rows, xw), rmap(0)), ho_spec]
    return pl.pallas_call(
        kern, grid=grid, in_specs=in_specs, out_specs=out_specs,
        out_shape=[jax.ShapeDtypeStruct((m, SSD_WIDTH), BF16),
                   jax.ShapeDtypeStruct((n_seq, SSD_HEADS, SSD_HEAD_DIM, D_STATE), F32)],
        scratch_shapes=scratch,
        compiler_params=pltpu.CompilerParams(dimension_semantics=sem, vmem_limit_bytes=VMEM_LIMIT),
        name="ssd_scan_carry" if carry else "ssd_scan_multi",
    )(ub, ub, ub, ub, usm, conv_w, conv_w, conv_w, conv_b, conv_b, conv_b, dtb, alog, expand, dsk, nw,
      conv0, conv0, conv0, h0)


def _matmul_kernel(x_ref, w_ref, o_ref, *, act):
    acc = _mm(x_ref[...], w_ref[...])
    if act == "relu2":
        acc = jnp.square(jnp.maximum(acc, 0.0))
    o_ref[...] = acc.astype(o_ref.dtype)


def _matmul(x, w, *, tm, tn, out_dtype, act=None, name):
    m, k = x.shape
    n = w.shape[1]
    return pl.pallas_call(
        functools.partial(_matmul_kernel, act=act),
        grid=(m // tm, n // tn),
        in_specs=[pl.BlockSpec((tm, k), lambda i, j: (i, 0)), pl.BlockSpec((k, tn), lambda i, j: (0, j))],
        out_specs=pl.BlockSpec((tm, tn), lambda i, j: (i, j)),
        out_shape=jax.ShapeDtypeStruct((m, n), out_dtype),
        compiler_params=pltpu.CompilerParams(dimension_semantics=("parallel", "parallel"),
                                             vmem_limit_bytes=VMEM_LIMIT),
        name=name,
    )(x, w)


def _out_proj_kernel(a_ref, b_ref, wa_ref, wb_ref, res_ref, o_ref):
    acc = _mm(a_ref[...], wa_ref[...]) + _mm(b_ref[...], wb_ref[...])
    o_ref[...] = ALPHA * res_ref[...] + acc


def _out_proj(a, b, w, res, *, tm, tn):
    m, kh = a.shape
    n = w.shape[1]
    return pl.pallas_call(
        _out_proj_kernel,
        grid=(m // tm, n // tn),
        in_specs=[pl.BlockSpec((tm, kh), lambda i, j: (i, 0)), pl.BlockSpec((tm, kh), lambda i, j: (i, 0)),
                  pl.BlockSpec((kh, tn), lambda i, j: (0, j)), pl.BlockSpec((kh, tn), lambda i, j: (1, j)),
                  pl.BlockSpec((tm, tn), lambda i, j: (i, j))],
        out_specs=pl.BlockSpec((tm, tn), lambda i, j: (i, j)),
        out_shape=jax.ShapeDtypeStruct((m, n), F32),
        compiler_params=pltpu.CompilerParams(dimension_semantics=("parallel", "parallel"),
                                             vmem_limit_bytes=VMEM_LIMIT),
        name="out_proj",
    )(a, b, w, w, res)


def _down_proj_kernel(x_ref, w_ref, res_ref, o_ref, acc_ref):
    kk = pl.program_id(2)

    @pl.when(kk == 0)
    def _():
        acc_ref[...] = ALPHA * res_ref[...]

    acc_ref[...] += _mm(x_ref[...], w_ref[...])

    @pl.when(kk == pl.num_programs(2) - 1)
    def _():
        o_ref[...] = acc_ref[...]


def _down_proj(x, w, res, *, tm, tn, tk):
    m, k = x.shape
    n = w.shape[1]
    return pl.pallas_call(
        _down_proj_kernel,
        grid=(m // tm, n // tn, k // tk),
        in_specs=[pl.BlockSpec((tm, tk), lambda i, j, kk: (i, kk)), pl.BlockSpec((tk, tn), lambda i, j, kk: (kk, j)),
                  pl.BlockSpec((tm, tn), lambda i, j, kk: (i, j))],
        out_specs=pl.BlockSpec((tm, tn), lambda i, j, kk: (i, j)),
        out_shape=jax.ShapeDtypeStruct((m, n), F32),
        scratch_shapes=[pltpu.VMEM((tm, tn), F32)],
        compiler_params=pltpu.CompilerParams(dimension_semantics=("parallel", "parallel", "arbitrary"),
                                             vmem_limit_bytes=VMEM_LIMIT),
        name="down_proj",
    )(x, w, res)


def _layer_norm_kernel(x_ref, g_ref, b_ref, *o_refs):
    x = x_ref[...]
    mu = jnp.mean(x, axis=-1, keepdims=True)
    xc = x - mu
    var = jnp.mean(jnp.square(xc), axis=-1, keepdims=True)
    y = xc * lax.rsqrt(var + EPS) * g_ref[...] + b_ref[...]
    for o_ref in o_refs:
        o_ref[...] = y.astype(o_ref.dtype)


def _layer_norm(x, g, b, *, tm, dtypes):
    m, d = x.shape
    return pl.pallas_call(
        _layer_norm_kernel,
        grid=(m // tm,),
        in_specs=[pl.BlockSpec((tm, d), lambda i: (i, 0)), pl.BlockSpec((1, d), lambda i: (0, 0)),
                  pl.BlockSpec((1, d), lambda i: (0, 0))],
        out_specs=[pl.BlockSpec((tm, d), lambda i: (i, 0)) for _ in dtypes],
        out_shape=[jax.ShapeDtypeStruct((m, d), dt) for dt in dtypes],
        compiler_params=pltpu.CompilerParams(dimension_semantics=("parallel",), vmem_limit_bytes=VMEM_LIMIT),
        name="layer_norm",
    )(x, g, b)


def _expand_matrix():
    e = np.zeros((SSD_GROUPS, LANE, GROUP_WIDTH), np.float32)
    for g in range(SSD_GROUPS):
        for j in range(HEADS_PER_GROUP):
            e[g, SM_DT + g * HEADS_PER_GROUP + j, j * SSD_HEAD_DIM:(j + 1) * SSD_HEAD_DIM] = 1.0
    return jnp.asarray(e, BF16)


def _lane_param(p):
    return jnp.zeros((1, LANE), F32).at[0, SM_DT:SM_DT + SSD_HEADS].set(p.astype(F32))


def _dense_tail(x, og, ys, prm, tm):
    pre1 = _out_proj(og, ys, prm["w_out"], x, tm=tm, tn=1024)
    h, hb = _layer_norm(pre1, prm["ln1_g"], prm["ln1_b"], tm=min(tm, 256), dtypes=(F32, BF16))
    hid = _matmul(hb, prm["w_up"], tm=tm, tn=1024, out_dtype=BF16, act="relu2", name="mlp_up")
    pre2 = _down_proj(hid, prm["w_down"], h, tm=tm, tn=1024, tk=2048)
    (out,) = _layer_norm(pre2, prm["ln2_g"], prm["ln2_b"], tm=min(tm, 256), dtypes=(F32,))
    return out


def kernel(x_prompt, x_sample, state_gla, state_ssm, state_conv, meta_tokens, w_in, w_gk_up, b_gk, gla_norm_w,
           conv_w, conv_b, dt_bias, a_log, d_skip, ssd_norm_w, w_out, ln1_g, ln1_b, w_up, w_down, ln2_g, ln2_b):
    bp, seq, d = x_prompt.shape
    bs, dec = x_sample.shape[:2]
    assert w_in.shape[0] == DEPTH and seq % CHUNK == 0 and dec == 4 and N_META <= CHUNK
    assert bs % SEQS_PER_BLOCK == 0
    wi = w_in[0]
    o_g = UA_WIDTH
    o_z = o_g + GATE_RANK
    o_dt = o_z + UB_WIDTH
    w_a = wi[:, :o_g].astype(BF16)
    w_b = wi[:, o_z:o_dt].astype(BF16)
    w_small = jnp.concatenate([wi[:, o_g:o_z], wi[:, o_dt:], jnp.zeros((d, LANE - GATE_RANK - SSD_HEADS), F32)],
                              axis=1).astype(BF16)
    wg = jnp.concatenate([w_gk_up[0], jnp.zeros((LANE - GATE_RANK, GLA_QK), F32)], axis=0).astype(BF16)
    prm = dict(
        w_out=w_out[0].astype(BF16), w_up=w_up[0].astype(BF16), w_down=w_down[0].astype(BF16),
        ln1_g=ln1_g[0][None], ln1_b=ln1_b[0][None], ln2_g=ln2_g[0][None], ln2_b=ln2_b[0][None],
    )
    gla_args = (wg, b_gk[0][None], gla_norm_w[0][None])
    ssd_args = (conv_w[0], conv_b[0][None], _lane_param(dt_bias[0]), _lane_param(a_log[0]), _expand_matrix(),
                jnp.repeat(d_skip[0], SSD_HEAD_DIM)[None, :], ssd_norm_w[0][None])

    def in_proj(xb, tm):
        ua = _matmul(xb, w_a, tm=tm, tn=1024, out_dtype=F32, name="in_proj_gla")
        ub = _matmul(xb, w_b, tm=tm, tn=1024, out_dtype=F32, name="in_proj_ssd")
        usm = _matmul(xb, w_small, tm=tm, tn=LANE, out_dtype=F32, name="in_proj_small")
        return ua, ub, usm

    lead_null = CHUNK - N_META
    xm = jnp.concatenate([jnp.zeros((lead_null, d), F32), meta_tokens], axis=0).astype(BF16)
    ua_m, ub_m, usm_m = in_proj(xm, CHUNK)
    _, gla_m = _gla_scan(ua_m, usm_m, *gla_args, jnp.zeros((1, GLA_HEADS, GLA_DK, GLA_DV), F32),
                         n_seq=1, seq_rows=CHUNK, block_rows=CHUNK, carry=True, lead_null=lead_null)
    _, ssm_m = _ssd_scan(ub_m, usm_m, *ssd_args, jnp.zeros((1, 8, CONV_DIM), F32),
                         jnp.zeros((1, SSD_HEADS, SSD_HEAD_DIM, D_STATE), F32),
                         n_seq=1, seq_rows=CHUNK, block_rows=CHUNK, carry=True, lead_null=lead_null)
    conv_m = ub_m[None, CHUNK - 8:, UB_X:]

    xp = x_prompt.reshape(bp * seq, d)
    ua_p, ub_p, usm_p = in_proj(xp.astype(BF16), 1024)
    og_p, gla_p = _gla_scan(ua_p, usm_p, *gla_args, gla_m, n_seq=bp, seq_rows=seq, block_rows=256, carry=True)
    ys_p, ssm_p = _ssd_scan(ub_p, usm_p, *ssd_args, conv_m, ssm_m, n_seq=bp, seq_rows=seq, block_rows=256,
                            carry=True)
    y_prompt = _dense_tail(xp, og_p, ys_p, prm, 1024).reshape(bp, seq, d)
    tail = CONV_W - 1
    conv_p = jnp.stack([ub_p[(b + 1) * seq - tail:(b + 1) * seq, UB_X:] for b in range(bp)])

    xs = x_sample.reshape(bs * dec, d)
    ua_s, ub_s, usm_s = in_proj(xs.astype(BF16), bs * dec)
    blk = SEQS_PER_BLOCK * dec
    og_s, gla_s = _gla_scan(ua_s, usm_s, *gla_args, state_gla[0], n_seq=bs, seq_rows=dec, block_rows=blk,
                            carry=False)
    ys_s, ssm_s = _ssd_scan(ub_s, usm_s, *ssd_args, state_conv[0], state_ssm[0], n_seq=bs, seq_rows=dec,
                            block_rows=blk, carry=False)
    y_sample = _dense_tail(xs, og_s, ys_s, prm, bs * dec).reshape(bs, dec, d)
    conv_s = ub_s[:, UB_X:].reshape(bs, dec, CONV_DIM)[:, dec - tail:]

    return (y_prompt, y_sample, gla_p[None], ssm_p[None], conv_p[None], gla_s[None], ssm_s[None], conv_s[None])
```

```python
import functools

import numpy as np
import jax
import jax.numpy as jnp
from jax import lax
from jax.experimental import pallas as pl
from jax.experimental.pallas import tpu as pltpu

F32 = jnp.float32
BF16 = jnp.bfloat16

GLA_HEADS = 4
GLA_DK = 256
GLA_DV = 512
GLA_QK = GLA_HEADS * GLA_DK
GLA_WIDTH = GLA_HEADS * GLA_DV
GATE_RANK = 16
GATE_NORM = 16.0
SSD_GROUPS = 8
HEADS_PER_GROUP = 4
SSD_HEAD_DIM = 64
SSD_HEADS = SSD_GROUPS * HEADS_PER_GROUP
GROUP_WIDTH = HEADS_PER_GROUP * SSD_HEAD_DIM
SSD_WIDTH = SSD_GROUPS * GROUP_WIDTH
D_STATE = 128
BC_WIDTH = SSD_GROUPS * D_STATE
CONV_W = 4
CONV_DIM = SSD_WIDTH + 2 * BC_WIDTH
N_META = 16
CHUNK = 64
CHUNK_SHIFT = 6
EPS = 1e-5
DEPTH = 1
ALPHA = (2.0 * DEPTH) ** 0.25

UA_K = GLA_QK
UA_V = 2 * GLA_QK
UA_R = UA_V + GLA_WIDTH
UA_WIDTH = UA_R + GLA_WIDTH
UB_X = SSD_WIDTH
UB_B = UB_X + SSD_WIDTH
UB_C = UB_B + BC_WIDTH
UB_WIDTH = UB_C + BC_WIDTH
LANE = 128
SM_DT = GATE_RANK

GLA_HEADS_PER_STEP = 4
SSD_GROUPS_PER_STEP = 4
SEQS_PER_BLOCK = 4

VMEM_LIMIT = 56 * 1024 * 1024


def _mm(a, b):
    return jnp.dot(a, b, preferred_element_type=F32)


def _nt(a, b):
    return lax.dot_general(a, b, (((1,), (1,)), ((), ())), preferred_element_type=F32)


def _split2(x):
    hi = x.astype(BF16)
    lo = (x - hi.astype(F32)).astype(BF16)
    return hi, lo


def _mm_split(m01, x):
    hi, lo = _split2(x)
    return _mm(m01, hi) + _mm(m01, lo)


def _pad_rows(x, n):
    t = x.shape[0]
    if t >= n:
        return x
    return jnp.concatenate([x, jnp.zeros((n - t, x.shape[1]), x.dtype)], axis=0)


def _silu(x):
    return x * (1.0 / (1.0 + jnp.exp(-x)))


def _softplus(x):
    return jnp.maximum(x, 0.0) + jnp.log(1.0 + jnp.exp(-jnp.abs(x)))


def _f01(m):
    return jnp.where(m, 1.0, 0.0).astype(BF16)


def _eye_bf16(n):
    ri = lax.broadcasted_iota(jnp.int32, (n, n), 0)
    ci = lax.broadcasted_iota(jnp.int32, (n, n), 1)
    return _f01(ri == ci)


def _split3_rows(x, period):
    hi = x.astype(BF16)
    r1 = x - hi.astype(F32)
    mid = r1.astype(BF16)
    lo = (r1 - mid.astype(F32)).astype(BF16)
    ridx = lax.broadcasted_iota(jnp.int32, x.shape, 0) & (period - 1)
    out = jnp.where(ridx == 0, hi.astype(F32),
                    jnp.where(ridx == 1, mid.astype(F32),
                              jnp.where(ridx == 2, lo.astype(F32), 0.0)))
    return out.astype(BF16)


def _lane_group_sum(t, shift, idx):
    lane = lax.broadcasted_iota(jnp.int32, t.shape, 1)
    return jnp.sum(jnp.where((lane >> shift) == idx, t, 0.0), axis=1, keepdims=True)


def _lane_group_keep(t, shift, idx):
    lane = lax.broadcasted_iota(jnp.int32, t.shape, 1)
    return jnp.where((lane >> shift) == idx, t, 0.0).astype(BF16)


def _chunk_masks(rows, cols, lead_null, lane_mod=None):
    ri = lax.broadcasted_iota(jnp.int32, (rows, cols), 0)
    ci = lax.broadcasted_iota(jnp.int32, (rows, cols), 1)
    if lane_mod is None:
        same = (ri >> CHUNK_SHIFT) == (ci >> CHUNK_SHIFT)
        cl = ci & (CHUNK - 1)
    else:
        cl = ci & lane_mod
        same = None
    rl = ri & (CHUNK - 1)
    valid = cl >= lead_null if lead_null else None
    base = same
    if valid is not None:
        base = valid if base is None else (base & valid)
    causal = (cl <= rl) if base is None else (base & (cl <= rl))
    middle = (lead_null + CHUNK) // 2 - 1
    upto_mid = (cl <= middle) if base is None else (base & (cl <= middle))
    return base, causal, upto_mid


def _seg_masks(tb, cols, seg_len, lane_mod=None):
    ri = lax.broadcasted_iota(jnp.int32, (tb, cols), 0)
    ci = lax.broadcasted_iota(jnp.int32, (tb, cols), 1)
    if lane_mod is not None:
        ci = ci & lane_mod
    shift = seg_len.bit_length() - 1
    same = ((ri >> shift) == (ci >> shift)) & (ci < tb)
    causal = same & (ci <= ri)
    upto_mid = same & ((ci & (seg_len - 1)) <= seg_len // 2 - 1)
    return same, causal, upto_mid


def _gla_gate(sm_b, wg, bg):
    gpre = _mm(sm_b, wg) + bg
    return -_softplus(-gpre) * (1.0 / GATE_NORM)


def _gla_finish(o, gnw, r):
    o = o * lax.rsqrt(jnp.mean(jnp.square(o), axis=-1, keepdims=True) + EPS)
    return o * gnw * _silu(r)


def _gla_block(q, k, v, g, mats, causal_p, eye):
    tri, midm, allm = mats
    rows, dk = q.shape
    rows_p = max(rows, LANE)
    bc = _mm_split(tri, g)
    anchor = _mm_split(midm, g)
    blast = _mm_split(allm, g)
    qs = q * (dk ** -0.5)
    qt = (qs * jnp.exp(bc - anchor)).astype(BF16)
    kt = _pad_rows((k * jnp.exp(anchor - bc)).astype(BF16), rows_p)
    p = jnp.where(causal_p, _nt(qt, kt), 0.0)
    vb = _pad_rows(v.astype(BF16), rows_p)
    o = _mm(p.astype(BF16), vb)
    qb = (qs * jnp.exp(bc)).astype(BF16)
    kd = _pad_rows((k * jnp.exp(blast - bc)).astype(BF16), rows_p)
    kd_t = _nt(eye, kd)
    return o, qb, kd_t, vb, jnp.exp(blast)


def _gla_carry_kernel(q_ref, k_ref, v_ref, r_ref, sm_ref, wg_ref, bg_ref, gnw_ref, s0_ref,
                      og_ref, s_ref, *, lead_null):
    rows = q_ref.shape[0]
    nchunk = rows // CHUNK
    rows_p = max(rows, LANE)

    @pl.when(pl.program_id(2) == 0)
    def _():
        s_ref[...] = s0_ref[...]

    eye = _eye_bf16(GLA_DK)
    same, causal, upto_mid = _chunk_masks(rows, rows, lead_null)
    mats = (_f01(causal), _f01(upto_mid), _f01(same))
    causal_p = _chunk_masks(rows, rows_p, lead_null)[1]
    sm_b = sm_ref[...].astype(BF16)
    for hh in range(GLA_HEADS_PER_STEP):
        ks = slice(hh * GLA_DK, (hh + 1) * GLA_DK)
        vs = slice(hh * GLA_DV, (hh + 1) * GLA_DV)
        g = _gla_gate(sm_b, wg_ref[:, ks], bg_ref[:, ks])
        k = k_ref[:, ks]
        if lead_null:
            rl = lax.broadcasted_iota(jnp.int32, (rows, 1), 0) & (CHUNK - 1)
            k = jnp.where(rl >= lead_null, k, 0.0)
        o, qb, kd_t, vb, e_last = _gla_block(q_ref[:, ks], k, v_ref[:, vs], g, mats, causal_p, eye)
        e_rows = [_split3_rows(e_last[c * CHUNK:c * CHUNK + 16], 16) for c in range(nchunk)]
        e_rows.append(jnp.zeros((LANE - 16 * nchunk, GLA_DK), BF16))
        e_t = _nt(eye, jnp.concatenate(e_rows, axis=0))
        s = s_ref[0, hh]
        inter = []
        for c in range(nchunk):
            inter.append(_mm(qb[c * CHUNK:(c + 1) * CHUNK], s.astype(BF16)))
            s = _lane_group_sum(e_t, 4, c) * s + _mm(_lane_group_keep(kd_t, CHUNK_SHIFT, c), vb)
        s_ref[0, hh] = s
        o = o + (inter[0] if nchunk == 1 else jnp.concatenate(inter, axis=0))
        og_ref[:, vs] = _gla_finish(o, gnw_ref[...], r_ref[:, vs]).astype(og_ref.dtype)


def _gla_multi_kernel(q_ref, k_ref, v_ref, r_ref, sm_ref, wg_ref, bg_ref, gnw_ref, s0_ref,
                      og_ref, s_ref, *, seg_len):
    tb = q_ref.shape[0]
    nseq = tb // seg_len
    seg_shift = seg_len.bit_length() - 1
    eye = _eye_bf16(GLA_DK)
    same, causal, upto_mid = _seg_masks(tb, tb, seg_len)
    mats = (_f01(causal), _f01(upto_mid), _f01(same))
    causal_p = _seg_masks(tb, LANE, seg_len)[1]
    sm_b = sm_ref[...].astype(BF16)
    for hh in range(GLA_HEADS_PER_STEP):
        ks = slice(hh * GLA_DK, (hh + 1) * GLA_DK)
        vs = slice(hh * GLA_DV, (hh + 1) * GLA_DV)
        g = _gla_gate(sm_b, wg_ref[:, ks], bg_ref[:, ks])
        o, qb, kd_t, vb, e_last = _gla_block(q_ref[:, ks], k_ref[:, ks], v_ref[:, vs], g, mats, causal_p, eye)
        e_rows = jnp.concatenate([_split3_rows(e_last, seg_len), jnp.zeros((LANE - tb, GLA_DK), BF16)], axis=0)
        e_t = _nt(eye, e_rows)
        rowseg = lax.broadcasted_iota(jnp.int32, (tb, 1), 0) >> seg_shift
        for i in range(nseq):
            s = s0_ref[i, hh]
            o = o + jnp.where(rowseg == i, _mm(qb, s.astype(BF16)), 0.0)
            s_ref[i, hh] = (_lane_group_sum(e_t, seg_shift, i) * s
                            + _mm(_lane_group_keep(kd_t, seg_shift, i), vb))
        og_ref[:, vs] = _gla_finish(o, gnw_ref[...], r_ref[:, vs]).astype(og_ref.dtype)


def _gla_scan(ua, usm, wg, bg, gnw, s0, *, n_seq, seq_rows, block_rows, carry, lead_null=0):
    m = ua.shape[0]
    nh = GLA_HEADS_PER_STEP
    kw, vw = nh * GLA_DK, nh * GLA_DV
    if carry:
        nb = seq_rows // block_rows
        grid = (n_seq, GLA_HEADS // nh, nb)
        s0_mul = 1 if s0.shape[0] == n_seq else 0
        rmap = lambda off: (lambda b, h, c: (b * nb + c, off + h))
        smap = lambda b, h, c: (b * nb + c, 0)
        cmap = lambda b, h, c: (0, h)
        zmap = lambda b, h, c: (0, 0)
        s0_spec = pl.BlockSpec((1, nh, GLA_DK, GLA_DV), lambda b, h, c: (b * s0_mul, h, 0, 0))
        so_spec = pl.BlockSpec((1, nh, GLA_DK, GLA_DV), lambda b, h, c: (b, h, 0, 0))
        sem = ("parallel", "parallel", "arbitrary")
        kern = functools.partial(_gla_carry_kernel, lead_null=lead_null)
    else:
        grid = (m // block_rows, GLA_HEADS // nh)
        rmap = lambda off: (lambda i, h: (i, off + h))
        smap = lambda i, h: (i, 0)
        cmap = lambda i, h: (0, h)
        zmap = lambda i, h: (0, 0)
        s0_spec = so_spec = pl.BlockSpec((SEQS_PER_BLOCK, nh, GLA_DK, GLA_DV), lambda i, h: (i, h, 0, 0))
        sem = ("parallel", "parallel")
        kern = functools.partial(_gla_multi_kernel, seg_len=seq_rows)
    in_specs = [
        pl.BlockSpec((block_rows, kw), rmap(0)), pl.BlockSpec((block_rows, kw), rmap(UA_K // kw)),
        pl.BlockSpec((block_rows, vw), rmap(UA_V // vw)), pl.BlockSpec((block_rows, vw), rmap(UA_R // vw)),
        pl.BlockSpec((block_rows, LANE), smap),
        pl.BlockSpec((LANE, kw), cmap), pl.BlockSpec((1, kw), cmap), pl.BlockSpec((1, GLA_DV), zmap),
        s0_spec,
    ]
    out_specs = [pl.BlockSpec((block_rows, vw), rmap(0)), so_spec]
    return pl.pallas_call(
        kern, grid=grid, in_specs=in_specs, out_specs=out_specs,
        out_shape=[jax.ShapeDtypeStruct((m, GLA_WIDTH), BF16),
                   jax.ShapeDtypeStruct((n_seq, GLA_HEADS, GLA_DK, GLA_DV), F32)],
        compiler_params=pltpu.CompilerParams(dimension_semantics=sem, vmem_limit_bytes=VMEM_LIMIT),
        name="gla_scan_carry" if carry else "gla_scan_multi",
    )(ua, ua, ua, ua, usm, wg, bg, gnw, s0)


def _ssd_block(x, dt, dta, expand, tri, allm, causal_l, eye, row_valid):
    rows, gw = x.shape
    rows_p = max(rows, LANE)
    p_hi, p_lo = [_mm(p, expand).astype(BF16) for p in _split2(dta)]
    cb = _mm(tri, p_hi) + _mm(tri, p_lo)
    cl = _mm(allm, p_hi) + _mm(allm, p_lo)
    ri = lax.broadcasted_iota(jnp.int32, (rows, gw), 0)
    li = lax.broadcasted_iota(jnp.int32, (rows, gw), 1)
    diag = jnp.where((li & (CHUNK - 1)) == (ri & (CHUNK - 1)), cb, 0.0)
    rw = _mm_split(allm, diag)
    decay = jnp.exp(jnp.where(causal_l, cb - rw, -jnp.inf))
    dt_hi, dt_lo = _split2(dt)
    xdt = x * (_mm(dt_hi, expand) + _mm(dt_lo, expand))
    if row_valid is not None:
        xdt = jnp.where(row_valid, xdt, 0.0)
    xw_t = _nt(eye, _pad_rows((xdt * jnp.exp(cl - cb)).astype(BF16), rows_p))
    return decay, xdt, jnp.exp(cb), xw_t, jnp.exp(cl)


def _ssd_intra(cm_b, bm_b, decay, xdt, seg_rows):
    gw = xdt.shape[1]
    b64 = _pad_rows(bm_b, CHUNK)
    cb4 = _nt(cm_b, jnp.concatenate([b64] * HEADS_PER_GROUP, axis=0))
    x64 = _pad_rows(xdt, CHUNK)
    r2 = lax.broadcasted_iota(jnp.int32, (gw, gw), 0)
    l2 = lax.broadcasted_iota(jnp.int32, (gw, gw), 1)
    bd = jnp.where((r2 >> CHUNK_SHIFT) == (l2 >> CHUNK_SHIFT),
                   jnp.concatenate([x64] * HEADS_PER_GROUP, axis=0), 0.0)
    return _mm((cb4 * decay).astype(BF16), bd.astype(BF16))


def _conv_rows(xpad, cw_refs, cb_refs, n):
    w = [jnp.concatenate([r[pl.ds(i, 1), :] for r in cw_refs], axis=1) for i in range(CONV_W)]
    bias = jnp.concatenate([r[...] for r in cb_refs], axis=1)
    xa = xpad[pl.ds(0, n + 8), :]
    x1 = pltpu.roll(xa, 1, axis=0)
    a = w[3] * xa + w[2] * x1
    b = w[1] * xa + w[0] * x1
    y = a + pltpu.roll(b, 2, axis=0) + bias
    return y[8:]


def _ssd_finish(y, x, dsk, z, nw):
    y = y + dsk * x
    y = y * _silu(z)
    y = y * lax.rsqrt(jnp.mean(jnp.square(y), axis=-1, keepdims=True) + EPS)
    return y * nw


def _ssd_lanes():
    ng = SSD_GROUPS_PER_STEP
    xw, sw = ng * GROUP_WIDTH, ng * D_STATE
    return ((0, xw), (xw, xw + sw), (xw + sw, xw + 2 * sw))


def _ssd_carry_kernel(x_ref, b_ref, c_ref, z_ref, sm_ref, cwx_ref, cwb_ref, cwc_ref, cbx_ref, cbb_ref, cbc_ref,
                      dtb_ref, alog_ref, ex_ref, dsk_ref, nw_ref, csx_ref, csb_ref, csc_ref, h0_ref,
                      ys_ref, h_ref, xpad, *, lead_null):
    rows = x_ref.shape[0]
    nchunk = rows // CHUNK
    rows_p = max(rows, LANE)
    gw = GROUP_WIDTH
    lanes = _ssd_lanes()

    @pl.when(pl.program_id(2) == 0)
    def _():
        h_ref[...] = h0_ref[...]
        for (lo, hi), cs in zip(lanes, (csx_ref, csb_ref, csc_ref)):
            xpad[pl.ds(0, 8), lo:hi] = cs[0]

    for (lo, hi), src in zip(lanes, (x_ref, b_ref, c_ref)):
        xpad[pl.ds(8, rows), lo:hi] = src[...]
    xc = _silu(_conv_rows(xpad, (cwx_ref, cwb_ref, cwc_ref), (cbx_ref, cbb_ref, cbc_ref), rows))
    xpad[pl.ds(0, 8), :] = xpad[pl.ds(rows, 8), :]

    eye = _eye_bf16(gw)
    same, causal, _ = _chunk_masks(rows, rows, lead_null)
    tri, allm = _f01(causal), _f01(same)
    causal_l = _chunk_masks(rows, gw, lead_null, lane_mod=CHUNK - 1)[1]
    row_valid = None
    if lead_null:
        row_valid = (lax.broadcasted_iota(jnp.int32, (rows, 1), 0) & (CHUNK - 1)) >= lead_null
    dt = _softplus(sm_ref[...] + dtb_ref[...])
    dta = dt * (-jnp.exp(alog_ref[...]))
    for gg in range(SSD_GROUPS_PER_STEP):
        xs = slice(gg * gw, (gg + 1) * gw)
        x = xc[:, lanes[0][0] + gg * gw:lanes[0][0] + (gg + 1) * gw]
        bm_b = xc[:, lanes[1][0] + gg * D_STATE:lanes[1][0] + (gg + 1) * D_STATE].astype(BF16)
        cm_b = xc[:, lanes[2][0] + gg * D_STATE:lanes[2][0] + (gg + 1) * D_STATE].astype(BF16)
        decay, xdt, ecb, xw_t, e_last = _ssd_block(x, dt, dta, ex_ref[gg], tri, allm, causal_l, eye, row_valid)
        bpad = _pad_rows(bm_b, rows_p)
        e_rows = [_split3_rows(e_last[c * CHUNK:c * CHUNK + 16], 16) for c in range(nchunk)]
        e_rows.append(jnp.zeros((LANE - 16 * nchunk, gw), BF16))
        e_t = _nt(eye, jnp.concatenate(e_rows, axis=0))
        hs = slice(gg * HEADS_PER_GROUP, (gg + 1) * HEADS_PER_GROUP)
        h = h_ref[0, hs].reshape(gw, D_STATE)
        ys = []
        for c in range(nchunk):
            rs = slice(c * CHUNK, (c + 1) * CHUNK)
            y = _ssd_intra(cm_b[rs], bm_b[rs], decay[rs], xdt[rs], CHUNK)
            ys.append(y + _nt(cm_b[rs], h.astype(BF16)) * ecb[rs])
            h = _lane_group_sum(e_t, 4, c) * h + _mm(_lane_group_keep(xw_t, CHUNK_SHIFT, c), bpad)
        h_ref[0, hs] = h.reshape(HEADS_PER_GROUP, SSD_HEAD_DIM, D_STATE)
        y = ys[0] if nchunk == 1 else jnp.concatenate(ys, axis=0)
        ys_ref[:, xs] = _ssd_finish(y, x, dsk_ref[:, xs], z_ref[:, xs], nw_ref[:, xs]).astype(ys_ref.dtype)


def _ssd_multi_kernel(x_ref, b_ref, c_ref, z_ref, sm_ref, cwx_ref, cwb_ref, cwc_ref, cbx_ref, cbb_ref, cbc_ref,
                      dtb_ref, alog_ref, ex_ref, dsk_ref, nw_ref, csx_ref, csb_ref, csc_ref, h0_ref,
                      ys_ref, h_ref, xpad, cres, xcs, *, seg_len):
    tb = x_ref.shape[0]
    nseq = tb // seg_len
    seg_shift = seg_len.bit_length() - 1
    gw = GROUP_WIDTH
    lanes = _ssd_lanes()
    xpad[...] = jnp.zeros(xpad.shape, F32)
    for i in range(nseq):
        for (lo, hi), src, cs in zip(lanes, (x_ref, b_ref, c_ref), (csx_ref, csb_ref, csc_ref)):
            xpad[pl.ds(8 + 8 * i + 1, CONV_W - 1), lo:hi] = cs[i]
            xpad[pl.ds(8 + 8 * i + 4, seg_len), lo:hi] = src[pl.ds(seg_len * i, seg_len), :]
    cres[...] = _conv_rows(xpad, (cwx_ref, cwb_ref, cwc_ref), (cbx_ref, cbb_ref, cbc_ref), 8 * nseq)
    for i in range(nseq):
        xcs[pl.ds(seg_len * i, seg_len), :] = cres[pl.ds(8 * i + 4, seg_len), :]
    xc = _silu(xcs[...])

    eye = _eye_bf16(gw)
    same, causal, _ = _seg_masks(tb, tb, seg_len)
    tri, allm = _f01(causal), _f01(same)
    causal_l = _seg_masks(tb, gw, seg_len, lane_mod=CHUNK - 1)[1]
    dt = _softplus(sm_ref[...] + dtb_ref[...])
    dta = dt * (-jnp.exp(alog_ref[...]))
    rowseg = lax.broadcasted_iota(jnp.int32, (tb, 1), 0) >> seg_shift
    for gg in range(SSD_GROUPS_PER_STEP):
        xs = slice(gg * gw, (gg + 1) * gw)
        x = xc[:, lanes[0][0] + gg * gw:lanes[0][0] + (gg + 1) * gw]
        bm_b = xc[:, lanes[1][0] + gg * D_STATE:lanes[1][0] + (gg + 1) * D_STATE].astype(BF16)
        cm_b = xc[:, lanes[2][0] + gg * D_STATE:lanes[2][0] + (gg + 1) * D_STATE].astype(BF16)
        decay, xdt, ecb, xw_t, e_last = _ssd_block(x, dt, dta, ex_ref[gg], tri, allm, causal_l, eye, None)
        bpad = _pad_rows(bm_b, LANE)
        e_rows = jnp.concatenate([_split3_rows(e_last, seg_len), jnp.zeros((LANE - tb, gw), BF16)], axis=0)
        e_t = _nt(eye, e_rows)
        y = _ssd_intra(cm_b, bm_b, decay, xdt, tb)
        hs = slice(gg * HEADS_PER_GROUP, (gg + 1) * HEADS_PER_GROUP)
        for i in range(nseq):
            h = h0_ref[i, hs].reshape(gw, D_STATE)
            y = y + jnp.where(rowseg == i, _nt(cm_b, h.astype(BF16)) * ecb, 0.0)
            h = _lane_group_sum(e_t, seg_shift, i) * h + _mm(_lane_group_keep(xw_t, seg_shift, i), bpad)
            h_ref[i, hs] = h.reshape(HEADS_PER_GROUP, SSD_HEAD_DIM, D_STATE)
        ys_ref[:, xs] = _ssd_finish(y, x, dsk_ref[:, xs], z_ref[:, xs], nw_ref[:, xs]).astype(ys_ref.dtype)


def _ssd_scan(ub, usm, conv_w, conv_b, dtb, alog, expand, dsk, nw, conv0, h0, *,
              n_seq, seq_rows, block_rows, carry, lead_null=0):
    m = ub.shape[0]
    ng = SSD_GROUPS_PER_STEP
    xw, sw = ng * GROUP_WIDTH, ng * D_STATE
    cw = xw + 2 * sw
    hpg = ng * HEADS_PER_GROUP
    if carry:
        nb = seq_rows // block_rows
        grid = (n_seq, SSD_GROUPS // ng, nb)
        c_mul = 1 if conv0.shape[0] == n_seq else 0
        h_mul = 1 if h0.shape[0] == n_seq else 0
        rmap = lambda off: (lambda b, g, c: (b * nb + c, off + g))
        smap = lambda b, g, c: (b * nb + c, 0)
        wmap = lambda off: (lambda b, g, c: (0, off + g))
        zmap = lambda b, g, c: (0, 0)
        exmap = lambda b, g, c: (g, 0, 0)
        cs_spec = lambda width, off: pl.BlockSpec((1, 8, width), lambda b, g, c: (b * c_mul, 0, off + g))
        h0_spec = pl.BlockSpec((1, hpg, SSD_HEAD_DIM, D_STATE), lambda b, g, c: (b * h_mul, g, 0, 0))
        ho_spec = pl.BlockSpec((1, hpg, SSD_HEAD_DIM, D_STATE), lambda b, g, c: (b, g, 0, 0))
        scratch = [pltpu.VMEM((block_rows + 8, cw), F32)]
        sem = ("parallel", "parallel", "arbitrary")
        kern = functools.partial(_ssd_carry_kernel, lead_null=lead_null)
    else:
        per = SEQS_PER_BLOCK
        grid = (m // block_rows, SSD_GROUPS // ng)
        rmap = lambda off: (lambda i, g: (i, off + g))
        smap = lambda i, g: (i, 0)
        wmap = lambda off: (lambda i, g: (0, off + g))
        zmap = lambda i, g: (0, 0)
        exmap = lambda i, g: (g, 0, 0)
        cs_spec = lambda width, off: pl.BlockSpec((per, CONV_W - 1, width), lambda i, g: (i, 0, off + g))
        h0_spec = ho_spec = pl.BlockSpec((per, hpg, SSD_HEAD_DIM, D_STATE), lambda i, g: (i, g, 0, 0))
        scratch = [pltpu.VMEM((8 * per + 8, cw), F32), pltpu.VMEM((8 * per, cw), F32),
                   pltpu.VMEM((block_rows, cw), F32)]
        sem = ("parallel", "parallel")
        kern = functools.partial(_ssd_multi_kernel, seg_len=seq_rows)
    cvb, cvc = SSD_WIDTH // sw, (SSD_WIDTH + BC_WIDTH) // sw
    in_specs = [
        pl.BlockSpec((block_rows, xw), rmap(UB_X // xw)), pl.BlockSpec((block_rows, sw), rmap(UB_B // sw)),
        pl.BlockSpec((block_rows, sw), rmap(UB_C // sw)), pl.BlockSpec((block_rows, xw), rmap(0)),
        pl.BlockSpec((block_rows, LANE), smap),
        pl.BlockSpec((CONV_W, xw), wmap(0)), pl.BlockSpec((CONV_W, sw), wmap(cvb)), pl.BlockSpec((CONV_W, sw), wmap(cvc)),
        pl.BlockSpec((1, xw), wmap(0)), pl.BlockSpec((1, sw), wmap(cvb)), pl.BlockSpec((1, sw), wmap(cvc)),
        pl.BlockSpec((1, LANE), zmap), pl.BlockSpec((1, LANE), zmap),
        pl.BlockSpec((ng, LANE, GROUP_WIDTH), exmap),
        pl.BlockSpec((1, xw), wmap(0)), pl.BlockSpec((1, xw), wmap(0)),
        cs_spec(xw, 0), cs_spec(sw, cvb), cs_spec(sw, cvc),
        h0_spec,
    ]
    out_specs = [pl.BlockSpec((block_rows, xw), rmap(0)), ho_spec]
    return pl.pallas_call(
        kern, grid=grid, in_specs=in_specs, out_specs=out_specs,
        out_shape=[jax.ShapeDtypeStruct((m, SSD_WIDTH), BF16),
                   jax.ShapeDtypeStruct((n_seq, SSD_HEADS, SSD_HEAD_DIM, D_STATE), F32)],
        scratch_shapes=scratch,
        compiler_params=pltpu.CompilerParams(dimension_semantics=sem, vmem_limit_bytes=VMEM_LIMIT),
        name="ssd_scan_carry" if carry else "ssd_scan_multi",
    )(ub, ub, ub, ub, usm, conv_w, conv_w, conv_w, conv_b, conv_b, conv_b, dtb, alog, expand, dsk, nw,
      conv0, conv0, conv0, h0)


def _matmul_kernel(x_ref, w_ref, o_ref, *wb_ref, act):
    w = w_ref[...]
    if wb_ref:
        w = w.astype(BF16)
        wb_ref[0][...] = w
    acc = _mm(x_ref[...], w)
    if act == "relu2":
        acc = jnp.square(jnp.maximum(acc, 0.0))
    o_ref[...] = acc.astype(o_ref.dtype)


def _matmul(x, w, *, tm, tn, out_dtype, act=None, name, n=None, emit=False):
    m, k = x.shape
    n = w.shape[1] if n is None else n
    assert not emit or m == tm
    out_specs = [pl.BlockSpec((tm, tn), lambda i, j: (i, j))]
    out_shape = [jax.ShapeDtypeStruct((m, n), out_dtype)]
    if emit:
        out_specs.append(pl.BlockSpec((k, tn), lambda i, j: (0, j)))
        out_shape.append(jax.ShapeDtypeStruct((k, n), BF16))
    res = pl.pallas_call(
        functools.partial(_matmul_kernel, act=act),
        grid=(m // tm, n // tn),
        in_specs=[pl.BlockSpec((tm, k), lambda i, j: (i, 0)), pl.BlockSpec((k, tn), lambda i, j: (0, j))],
        out_specs=out_specs, out_shape=out_shape,
        compiler_params=pltpu.CompilerParams(dimension_semantics=("parallel", "parallel"),
                                             vmem_limit_bytes=VMEM_LIMIT),
        name=name,
    )(x, w)
    return res if emit else res[0]


def _out_proj_kernel(a_ref, b_ref, wa_ref, wb_ref, res_ref, o_ref, *wo_refs):
    wa, wb = wa_ref[...], wb_ref[...]
    if wo_refs:
        wa, wb = wa.astype(BF16), wb.astype(BF16)
        wo_refs[0][...] = wa
        wo_refs[1][...] = wb
    o_ref[...] = ALPHA * res_ref[...] + _mm(a_ref[...], wa) + _mm(b_ref[...], wb)


def _out_proj(a, b, w_top, w_bot, res, *, tm, tn, emit=False):
    m, kh = a.shape
    n = w_top.shape[1]
    assert not emit or m == tm
    out_specs = [pl.BlockSpec((tm, tn), lambda i, j: (i, j))]
    out_shape = [jax.ShapeDtypeStruct((m, n), F32)]
    if emit:
        out_specs += [pl.BlockSpec((kh, tn), lambda i, j: (0, j))] * 2
        out_shape += [jax.ShapeDtypeStruct((kh, n), BF16)] * 2
    res_ = pl.pallas_call(
        _out_proj_kernel,
        grid=(m // tm, n // tn),
        in_specs=[pl.BlockSpec((tm, kh), lambda i, j: (i, 0)), pl.BlockSpec((tm, kh), lambda i, j: (i, 0)),
                  pl.BlockSpec((kh, tn), lambda i, j: (0, j)),
                  pl.BlockSpec((kh, tn), lambda i, j: (1 if emit else 0, j)),
                  pl.BlockSpec((tm, tn), lambda i, j: (i, j))],
        out_specs=out_specs, out_shape=out_shape,
        compiler_params=pltpu.CompilerParams(dimension_semantics=("parallel", "parallel"),
                                             vmem_limit_bytes=VMEM_LIMIT),
        name="out_proj",
    )(a, b, w_top, w_bot, res)
    return res_ if emit else res_[0]


def _down_proj_kernel(x_ref, w_ref, res_ref, o_ref, *rest):
    acc_ref = rest[-1]
    kk = pl.program_id(2)

    @pl.when(kk == 0)
    def _():
        acc_ref[...] = ALPHA * res_ref[...]

    w = w_ref[...]
    if len(rest) == 2:
        w = w.astype(BF16)
        rest[0][...] = w
    acc_ref[...] += _mm(x_ref[...], w)

    @pl.when(kk == pl.num_programs(2) - 1)
    def _():
        o_ref[...] = acc_ref[...]


def _down_proj(x, w, res, *, tm, tn, tk, emit=False):
    m, k = x.shape
    n = w.shape[1]
    assert not emit or m == tm
    out_specs = [pl.BlockSpec((tm, tn), lambda i, j, kk: (i, j))]
    out_shape = [jax.ShapeDtypeStruct((m, n), F32)]
    if emit:
        out_specs.append(pl.BlockSpec((tk, tn), lambda i, j, kk: (kk, j)))
        out_shape.append(jax.ShapeDtypeStruct((k, n), BF16))
    res_ = pl.pallas_call(
        _down_proj_kernel,
        grid=(m // tm, n // tn, k // tk),
        in_specs=[pl.BlockSpec((tm, tk), lambda i, j, kk: (i, kk)), pl.BlockSpec((tk, tn), lambda i, j, kk: (kk, j)),
                  pl.BlockSpec((tm, tn), lambda i, j, kk: (i, j))],
        out_specs=out_specs, out_shape=out_shape,
        scratch_shapes=[pltpu.VMEM((tm, tn), F32)],
        compiler_params=pltpu.CompilerParams(dimension_semantics=("parallel", "parallel", "arbitrary"),
                                             vmem_limit_bytes=VMEM_LIMIT),
        name="down_proj",
    )(x, w, res)
    return res_ if emit else res_[0]


def _layer_norm_kernel(x_ref, g_ref, b_ref, *o_refs):
    x = x_ref[...]
    mu = jnp.mean(x, axis=-1, keepdims=True)
    xc = x - mu
    var = jnp.mean(jnp.square(xc), axis=-1, keepdims=True)
    y = xc * lax.rsqrt(var + EPS) * g_ref[...] + b_ref[...]
    for o_ref in o_refs:
        o_ref[...] = y.astype(o_ref.dtype)


def _layer_norm(x, g, b, *, tm, dtypes):
    m, d = x.shape
    return pl.pallas_call(
        _layer_norm_kernel,
        grid=(m // tm,),
        in_specs=[pl.BlockSpec((tm, d), lambda i: (i, 0)), pl.BlockSpec((1, d), lambda i: (0, 0)),
                  pl.BlockSpec((1, d), lambda i: (0, 0))],
        out_specs=[pl.BlockSpec((tm, d), lambda i: (i, 0)) for _ in dtypes],
        out_shape=[jax.ShapeDtypeStruct((m, d), dt) for dt in dtypes],
        compiler_params=pltpu.CompilerParams(dimension_semantics=("parallel",), vmem_limit_bytes=VMEM_LIMIT),
        name="layer_norm",
    )(x, g, b)


def _expand_matrix():
    e = np.zeros((SSD_GROUPS, LANE, GROUP_WIDTH), np.float32)
    for g in range(SSD_GROUPS):
        for j in range(HEADS_PER_GROUP):
            e[g, SM_DT + g * HEADS_PER_GROUP + j, j * SSD_HEAD_DIM:(j + 1) * SSD_HEAD_DIM] = 1.0
    return jnp.asarray(e, BF16)


def _lane_param(p):
    return jnp.zeros((1, LANE), F32).at[0, SM_DT:SM_DT + SSD_HEADS].set(p.astype(F32))


def _dense_tail(x, og, ys, prm, tm, emit):
    wb = dict(prm)
    tn = 512 if emit else 1024
    w_top, w_bot = (prm["w_out"], prm["w_out"]) if emit else prm["w_out"]
    pre1 = _out_proj(og, ys, w_top, w_bot, x, tm=tm, tn=tn, emit=emit)
    if emit:
        pre1, wt, wbt = pre1
        wb["w_out"] = (wt, wbt)
    h, hb = _layer_norm(pre1, prm["ln1_g"], prm["ln1_b"], tm=min(tm, 256), dtypes=(F32, BF16))
    hid = _matmul(hb, prm["w_up"], tm=tm, tn=tn, out_dtype=BF16, act="relu2", name="mlp_up", emit=emit)
    if emit:
        hid, wb["w_up"] = hid
    pre2 = _down_proj(hid, prm["w_down"], h, tm=tm, tn=1024, tk=2048, emit=emit)
    if emit:
        pre2, wb["w_down"] = pre2
    (out,) = _layer_norm(pre2, prm["ln2_g"], prm["ln2_b"], tm=min(tm, 256), dtypes=(F32,))
    return out, wb


def kernel(x_prompt, x_sample, state_gla, state_ssm, state_conv, meta_tokens, w_in, w_gk_up, b_gk, gla_norm_w,
           conv_w, conv_b, dt_bias, a_log, d_skip, ssd_norm_w, w_out, ln1_g, ln1_b, w_up, w_down, ln2_g, ln2_b):
    bp, seq, d = x_prompt.shape
    bs, dec = x_sample.shape[:2]
    assert w_in.shape[0] == DEPTH and seq % CHUNK == 0 and dec == 4 and N_META <= CHUNK
    assert bs % SEQS_PER_BLOCK == 0
    wi = w_in[0]
    o_g = UA_WIDTH
    o_z = o_g + GATE_RANK
    o_dt = o_z + UB_WIDTH
    w_b = wi[:, o_z:o_dt].astype(BF16)
    w_small = jnp.concatenate([wi[:, o_g:o_z], wi[:, o_dt:], jnp.zeros((d, LANE - GATE_RANK - SSD_HEADS), F32)],
                              axis=1).astype(BF16)
    wg = jnp.concatenate([w_gk_up[0], jnp.zeros((LANE - GATE_RANK, GLA_QK), F32)], axis=0).astype(BF16)
    prm = dict(
        w_out=w_out[0], w_up=w_up[0], w_down=w_down[0],
        ln1_g=ln1_g[0][None], ln1_b=ln1_b[0][None], ln2_g=ln2_g[0][None], ln2_b=ln2_b[0][None],
    )
    gla_args = (wg, b_gk[0][None], gla_norm_w[0][None])
    ssd_args = (conv_w[0], conv_b[0][None], _lane_param(dt_bias[0]), _lane_param(a_log[0]), _expand_matrix(),
                jnp.repeat(d_skip[0], SSD_HEAD_DIM)[None, :], ssd_norm_w[0][None])
    tail = CONV_W - 1

    def in_proj(xb, tm, w_a):
        if w_a is None:
            ua, w_a = _matmul(xb, wi, tm=tm, tn=512, out_dtype=F32, name="in_proj_gla", n=UA_WIDTH, emit=True)
        else:
            ua = _matmul(xb, w_a, tm=tm, tn=1024, out_dtype=F32, name="in_proj_gla")
        ub = _matmul(xb, w_b, tm=tm, tn=1024, out_dtype=F32, name="in_proj_ssd")
        usm = _matmul(xb, w_small, tm=tm, tn=LANE, out_dtype=F32, name="in_proj_small")
        return ua, ub, usm, w_a

    xs = x_sample.reshape(bs * dec, d)
    ua_s, ub_s, usm_s, w_a = in_proj(xs.astype(BF16), bs * dec, None)
    blk = SEQS_PER_BLOCK * dec
    og_s, gla_s = _gla_scan(ua_s, usm_s, *gla_args, state_gla[0], n_seq=bs, seq_rows=dec, block_rows=blk,
                            carry=False)
    ys_s, ssm_s = _ssd_scan(ub_s, usm_s, *ssd_args, state_conv[0], state_ssm[0], n_seq=bs, seq_rows=dec,
                            block_rows=blk, carry=False)
    y_sample, prm_b = _dense_tail(xs, og_s, ys_s, prm, bs * dec, True)
    y_sample = y_sample.reshape(bs, dec, d)
    conv_s = ub_s[:, UB_X:].reshape(bs, dec, CONV_DIM)[:, dec - tail:]

    lead_null = CHUNK - N_META
    xm = jnp.concatenate([jnp.zeros((lead_null, d), F32), meta_tokens], axis=0).astype(BF16)
    ua_m, ub_m, usm_m, _ = in_proj(xm, CHUNK, w_a)
    _, gla_m = _gla_scan(ua_m, usm_m, *gla_args, jnp.zeros((1, GLA_HEADS, GLA_DK, GLA_DV), F32),
                         n_seq=1, seq_rows=CHUNK, block_rows=CHUNK, carry=True, lead_null=lead_null)
    _, ssm_m = _ssd_scan(ub_m, usm_m, *ssd_args, jnp.zeros((1, 8, CONV_DIM), F32),
                         jnp.zeros((1, SSD_HEADS, SSD_HEAD_DIM, D_STATE), F32),
                         n_seq=1, seq_rows=CHUNK, block_rows=CHUNK, carry=True, lead_null=lead_null)
    conv_m = ub_m[None, CHUNK - 8:, UB_X:]

    xp = x_prompt.reshape(bp * seq, d)
    ua_p, ub_p, usm_p, _ = in_proj(xp.astype(BF16), 1024, w_a)
    og_p, gla_p = _gla_scan(ua_p, usm_p, *gla_args, gla_m, n_seq=bp, seq_rows=seq, block_rows=256, carry=True)
    ys_p, ssm_p = _ssd_scan(ub_p, usm_p, *ssd_args, conv_m, ssm_m, n_seq=bp, seq_rows=seq, block_rows=256,
                            carry=True)
    y_prompt, _ = _dense_tail(xp, og_p, ys_p, prm_b, 1024, False)
    y_prompt = y_prompt.reshape(bp, seq, d)
    conv_p = jnp.stack([ub_p[(b + 1) * seq - tail:(b + 1) * seq, UB_X:] for b in range(bp)])

    return (y_prompt, y_sample, gla_p[None], ssm_p[None], conv_p[None], gla_s[None], ssm_s[None], conv_s[None])
```

```python
import functools

import numpy as np
import jax
import jax.numpy as jnp
from jax import lax
from jax.experimental import pallas as pl
from jax.experimental.pallas import tpu as pltpu

F32 = jnp.float32
BF16 = jnp.bfloat16

GLA_HEADS = 4
GLA_DK = 256
GLA_DV = 512
GLA_QK = GLA_HEADS * GLA_DK
GLA_WIDTH = GLA_HEADS * GLA_DV
GATE_RANK = 16
GATE_NORM = 16.0
SSD_GROUPS = 8
HEADS_PER_GROUP = 4
SSD_HEAD_DIM = 64
SSD_HEADS = SSD_GROUPS * HEADS_PER_GROUP
GROUP_WIDTH = HEADS_PER_GROUP * SSD_HEAD_DIM
SSD_WIDTH = SSD_GROUPS * GROUP_WIDTH
D_STATE = 128
BC_WIDTH = SSD_GROUPS * D_STATE
CONV_W = 4
CONV_DIM = SSD_WIDTH + 2 * BC_WIDTH
N_META = 16
CHUNK = 64
CHUNK_SHIFT = 6
EPS = 1e-5
DEPTH = 1
ALPHA = (2.0 * DEPTH) ** 0.25

UA_K = GLA_QK
UA_V = 2 * GLA_QK
UA_R = UA_V + GLA_WIDTH
UA_WIDTH = UA_R + GLA_WIDTH
UB_X = SSD_WIDTH
UB_B = UB_X + SSD_WIDTH
UB_C = UB_B + BC_WIDTH
UB_WIDTH = UB_C + BC_WIDTH
LANE = 128
SM_DT = GATE_RANK

GLA_HEADS_PER_STEP = 4
SSD_GROUPS_PER_STEP = 4
SEQS_PER_BLOCK = 4

VMEM_LIMIT = 56 * 1024 * 1024


def _mm(a, b):
    return jnp.dot(a, b, preferred_element_type=F32)


def _nt(a, b):
    return lax.dot_general(a, b, (((1,), (1,)), ((), ())), preferred_element_type=F32)


def _split2(x):
    hi = x.astype(BF16)
    lo = (x - hi.astype(F32)).astype(BF16)
    return hi, lo


def _mm_split(m01, x):
    hi, lo = _split2(x)
    return _mm(m01, hi) + _mm(m01, lo)


def _pad_rows(x, n):
    t = x.shape[0]
    if t >= n:
        return x
    return jnp.concatenate([x, jnp.zeros((n - t, x.shape[1]), x.dtype)], axis=0)


def _silu(x):
    return x * (1.0 / (1.0 + jnp.exp(-x)))


def _softplus(x):
    return jnp.maximum(x, 0.0) + jnp.log(1.0 + jnp.exp(-jnp.abs(x)))


def _f01(m):
    return jnp.where(m, 1.0, 0.0).astype(BF16)


def _eye_bf16(n):
    ri = lax.broadcasted_iota(jnp.int32, (n, n), 0)
    ci = lax.broadcasted_iota(jnp.int32, (n, n), 1)
    return _f01(ri == ci)


def _split3_rows(x, period):
    hi = x.astype(BF16)
    r1 = x - hi.astype(F32)
    mid = r1.astype(BF16)
    lo = (r1 - mid.astype(F32)).astype(BF16)
    ridx = lax.broadcasted_iota(jnp.int32, x.shape, 0) & (period - 1)
    out = jnp.where(ridx == 0, hi.astype(F32),
                    jnp.where(ridx == 1, mid.astype(F32),
                              jnp.where(ridx == 2, lo.astype(F32), 0.0)))
    return out.astype(BF16)


def _lane_group_sum(t, shift, idx):
    lane = lax.broadcasted_iota(jnp.int32, t.shape, 1)
    return jnp.sum(jnp.where((lane >> shift) == idx, t, 0.0), axis=1, keepdims=True)


def _chunk_rows_only(xb, c):
    rows_p, w = xb.shape
    lo, hi = c * CHUNK, (c + 1) * CHUNK
    pieces = []
    if lo:
        pieces.append(jnp.zeros((lo, w), xb.dtype))
    pieces.append(xb[lo:hi])
    if hi < rows_p:
        pieces.append(jnp.zeros((rows_p - hi, w), xb.dtype))
    return pieces[0] if len(pieces) == 1 else jnp.concatenate(pieces, axis=0)


def _lane_group_keep(t, shift, idx):
    lane = lax.broadcasted_iota(jnp.int32, t.shape, 1)
    return jnp.where((lane >> shift) == idx, t, 0.0).astype(BF16)


def _chunk_masks(rows, cols, lead_null, lane_mod=None):
    ri = lax.broadcasted_iota(jnp.int32, (rows, cols), 0)
    ci = lax.broadcasted_iota(jnp.int32, (rows, cols), 1)
    if lane_mod is None:
        same = (ri >> CHUNK_SHIFT) == (ci >> CHUNK_SHIFT)
        cl = ci & (CHUNK - 1)
    else:
        cl = ci & lane_mod
        same = None
    rl = ri & (CHUNK - 1)
    valid = cl >= lead_null if lead_null else None
    base = same
    if valid is not None:
        base = valid if base is None else (base & valid)
    causal = (cl <= rl) if base is None else (base & (cl <= rl))
    middle = (lead_null + CHUNK) // 2 - 1
    upto_mid = (cl <= middle) if base is None else (base & (cl <= middle))
    return base, causal, upto_mid


def _seg_masks(tb, cols, seg_len, lane_mod=None):
    ri = lax.broadcasted_iota(jnp.int32, (tb, cols), 0)
    ci = lax.broadcasted_iota(jnp.int32, (tb, cols), 1)
    if lane_mod is not None:
        ci = ci & lane_mod
    shift = seg_len.bit_length() - 1
    same = ((ri >> shift) == (ci >> shift)) & (ci < tb)
    causal = same & (ci <= ri)
    upto_mid = same & ((ci & (seg_len - 1)) <= seg_len // 2 - 1)
    return same, causal, upto_mid


def _gla_gate(sm_b, wg, bg):
    gpre = _mm(sm_b, wg) + bg
    return -_softplus(-gpre) * (1.0 / GATE_NORM)


def _gla_finish(o, gnw, r):
    o = o * lax.rsqrt(jnp.mean(jnp.square(o), axis=-1, keepdims=True) + EPS)
    return o * gnw * _silu(r)


def _gla_block(q, k, v, g, mats, causal_p, eye):
    tri, midm, allm = mats
    rows, dk = q.shape
    rows_p = max(rows, LANE)
    bc = _mm_split(tri, g)
    anchor = _mm_split(midm, g)
    blast = _mm_split(allm, g)
    qs = q * (dk ** -0.5)
    qt = (qs * jnp.exp(bc - anchor)).astype(BF16)
    kt = _pad_rows((k * jnp.exp(anchor - bc)).astype(BF16), rows_p)
    p = jnp.where(causal_p, _nt(qt, kt), 0.0)
    vb = _pad_rows(v.astype(BF16), rows_p)
    o = _mm(p.astype(BF16), vb)
    qb = (qs * jnp.exp(bc)).astype(BF16)
    kd = _pad_rows((k * jnp.exp(blast - bc)).astype(BF16), rows_p)
    kd_t = _nt(eye, kd)
    return o, qb, kd_t, vb, jnp.exp(blast)


def _gla_carry_kernel(q_ref, k_ref, v_ref, r_ref, sm_ref, wg_ref, bg_ref, gnw_ref, s0_ref,
                      og_ref, s_ref, *, lead_null):
    rows = q_ref.shape[0]
    nchunk = rows // CHUNK
    rows_p = max(rows, LANE)

    @pl.when(pl.program_id(2) == 0)
    def _():
        s_ref[...] = s0_ref[...]

    eye = _eye_bf16(GLA_DK)
    same, causal, upto_mid = _chunk_masks(rows, rows, lead_null)
    mats = (_f01(causal), _f01(upto_mid), _f01(same))
    causal_p = _chunk_masks(rows, rows_p, lead_null)[1]
    sm_b = sm_ref[...].astype(BF16)
    for hh in range(GLA_HEADS_PER_STEP):
        ks = slice(hh * GLA_DK, (hh + 1) * GLA_DK)
        vs = slice(hh * GLA_DV, (hh + 1) * GLA_DV)
        g = _gla_gate(sm_b, wg_ref[:, ks], bg_ref[:, ks])
        k = k_ref[:, ks]
        if lead_null:
            rl = lax.broadcasted_iota(jnp.int32, (rows, 1), 0) & (CHUNK - 1)
            k = jnp.where(rl >= lead_null, k, 0.0)
        o, qb, kd_t, vb, e_last = _gla_block(q_ref[:, ks], k, v_ref[:, vs], g, mats, causal_p, eye)
        e_rows = [_split3_rows(e_last[c * CHUNK:c * CHUNK + 16], 16) for c in range(nchunk)]
        e_rows.append(jnp.zeros((LANE - 16 * nchunk, GLA_DK), BF16))
        e_t = _nt(eye, jnp.concatenate(e_rows, axis=0))
        s = s_ref[0, hh]
        kd_tb = kd_t.astype(BF16)
        inter = []
        for c in range(nchunk):
            inter.append(_mm(qb[c * CHUNK:(c + 1) * CHUNK], s.astype(BF16)))
            s = _lane_group_sum(e_t, 4, c) * s + _mm(kd_tb, _chunk_rows_only(vb, c))
        s_ref[0, hh] = s
        o = o + (inter[0] if nchunk == 1 else jnp.concatenate(inter, axis=0))
        og_ref[:, vs] = _gla_finish(o, gnw_ref[...], r_ref[:, vs]).astype(og_ref.dtype)


def _gla_multi_kernel(q_ref, k_ref, v_ref, r_ref, sm_ref, wg_ref, bg_ref, gnw_ref, s0_ref,
                      og_ref, s_ref, *, seg_len):
    tb = q_ref.shape[0]
    nseq = tb // seg_len
    seg_shift = seg_len.bit_length() - 1
    eye = _eye_bf16(GLA_DK)
    same, causal, upto_mid = _seg_masks(tb, tb, seg_len)
    mats = (_f01(causal), _f01(upto_mid), _f01(same))
    causal_p = _seg_masks(tb, LANE, seg_len)[1]
    sm_b = sm_ref[...].astype(BF16)
    for hh in range(GLA_HEADS_PER_STEP):
        ks = slice(hh * GLA_DK, (hh + 1) * GLA_DK)
        vs = slice(hh * GLA_DV, (hh + 1) * GLA_DV)
        g = _gla_gate(sm_b, wg_ref[:, ks], bg_ref[:, ks])
        o, qb, kd_t, vb, e_last = _gla_block(q_ref[:, ks], k_ref[:, ks], v_ref[:, vs], g, mats, causal_p, eye)
        e_rows = jnp.concatenate([_split3_rows(e_last, seg_len), jnp.zeros((LANE - tb, GLA_DK), BF16)], axis=0)
        e_t = _nt(eye, e_rows)
        rowseg = lax.broadcasted_iota(jnp.int32, (tb, 1), 0) >> seg_shift
        for i in range(nseq):
            s = s0_ref[i, hh]
            o = o + jnp.where(rowseg == i, _mm(qb, s.astype(BF16)), 0.0)
            s_ref[i, hh] = (_lane_group_sum(e_t, seg_shift, i) * s
                            + _mm(_lane_group_keep(kd_t, seg_shift, i), vb))
        og_ref[:, vs] = _gla_finish(o, gnw_ref[...], r_ref[:, vs]).astype(og_ref.dtype)


def _gla_scan(ua, usm, wg, bg, gnw, s0, *, n_seq, seq_rows, block_rows, carry, lead_null=0):
    m = ua.shape[0]
    nh = GLA_HEADS_PER_STEP
    kw, vw = nh * GLA_DK, nh * GLA_DV
    if carry:
        nb = seq_rows // block_rows
        grid = (n_seq, GLA_HEADS // nh, nb)
        s0_mul = 1 if s0.shape[0] == n_seq else 0
        rmap = lambda off: (lambda b, h, c: (b * nb + c, off + h))
        smap = lambda b, h, c: (b * nb + c, 0)
        cmap = lambda b, h, c: (0, h)
        zmap = lambda b, h, c: (0, 0)
        s0_spec = pl.BlockSpec((1, nh, GLA_DK, GLA_DV), lambda b, h, c: (b * s0_mul, h, 0, 0))
        so_spec = pl.BlockSpec((1, nh, GLA_DK, GLA_DV), lambda b, h, c: (b, h, 0, 0))
        sem = ("parallel", "parallel", "arbitrary")
        kern = functools.partial(_gla_carry_kernel, lead_null=lead_null)
    else:
        grid = (m // block_rows, GLA_HEADS // nh)
        rmap = lambda off: (lambda i, h: (i, off + h))
        smap = lambda i, h: (i, 0)
        cmap = lambda i, h: (0, h)
        zmap = lambda i, h: (0, 0)
        s0_spec = so_spec = pl.BlockSpec((SEQS_PER_BLOCK, nh, GLA_DK, GLA_DV), lambda i, h: (i, h, 0, 0))
        sem = ("parallel", "parallel")
        kern = functools.partial(_gla_multi_kernel, seg_len=seq_rows)
    in_specs = [
        pl.BlockSpec((block_rows, kw), rmap(0)), pl.BlockSpec((block_rows, kw), rmap(UA_K // kw)),
        pl.BlockSpec((block_rows, vw), rmap(UA_V // vw)), pl.BlockSpec((block_rows, vw), rmap(UA_R // vw)),
        pl.BlockSpec((block_rows, LANE), smap),
        pl.BlockSpec((LANE, kw), cmap), pl.BlockSpec((1, kw), cmap), pl.BlockSpec((1, GLA_DV), zmap),
        s0_spec,
    ]
    out_specs = [pl.BlockSpec((block_rows, vw), rmap(0)), so_spec]
    return pl.pallas_call(
        kern, grid=grid, in_specs=in_specs, out_specs=out_specs,
        out_shape=[jax.ShapeDtypeStruct((m, GLA_WIDTH), BF16),
                   jax.ShapeDtypeStruct((n_seq, GLA_HEADS, GLA_DK, GLA_DV), F32)],
        compiler_params=pltpu.CompilerParams(dimension_semantics=sem, vmem_limit_bytes=VMEM_LIMIT),
        name="gla_scan_carry" if carry else "gla_scan_multi",
    )(ua, ua, ua, ua, usm, wg, bg, gnw, s0)


def _ssd_block(x, dt, dta, expand, tri, allm, causal_l, eye, row_valid):
    rows, gw = x.shape
    rows_p = max(rows, LANE)
    p_hi, p_lo = [_mm(p, expand).astype(BF16) for p in _split2(dta)]
    cb = _mm(tri, p_hi) + _mm(tri, p_lo)
    cl = _mm(allm, p_hi) + _mm(allm, p_lo)
    ri = lax.broadcasted_iota(jnp.int32, (rows, gw), 0)
    li = lax.broadcasted_iota(jnp.int32, (rows, gw), 1)
    diag = jnp.where((li & (CHUNK - 1)) == (ri & (CHUNK - 1)), cb, 0.0)
    rw = _mm_split(allm, diag)
    decay = jnp.exp(jnp.where(causal_l, cb - rw, -jnp.inf))
    dt_hi, dt_lo = _split2(dt)
    xdt = x * (_mm(dt_hi, expand) + _mm(dt_lo, expand))
    if row_valid is not None:
        xdt = jnp.where(row_valid, xdt, 0.0)
    xw_t = _nt(eye, _pad_rows((xdt * jnp.exp(cl - cb)).astype(BF16), rows_p))
    return decay, xdt, jnp.exp(cb), xw_t, jnp.exp(cl)


def _ssd_intra(cm_b, bm_b, decay, xdt, seg_rows):
    gw = xdt.shape[1]
    b64 = _pad_rows(bm_b, CHUNK)
    cb4 = _nt(cm_b, jnp.concatenate([b64] * HEADS_PER_GROUP, axis=0))
    x64 = _pad_rows(xdt, CHUNK)
    r2 = lax.broadcasted_iota(jnp.int32, (gw, gw), 0)
    l2 = lax.broadcasted_iota(jnp.int32, (gw, gw), 1)
    bd = jnp.where((r2 >> CHUNK_SHIFT) == (l2 >> CHUNK_SHIFT),
                   jnp.concatenate([x64] * HEADS_PER_GROUP, axis=0), 0.0)
    return _mm((cb4 * decay).astype(BF16), bd.astype(BF16))


def _conv_rows(xpad, cw_refs, cb_refs, n):
    w = [jnp.concatenate([r[pl.ds(i, 1), :] for r in cw_refs], axis=1) for i in range(CONV_W)]
    bias = jnp.concatenate([r[...] for r in cb_refs], axis=1)
    xa = xpad[pl.ds(0, n + 8), :]
    x1 = pltpu.roll(xa, 1, axis=0)
    a = w[3] * xa + w[2] * x1
    b = w[1] * xa + w[0] * x1
    y = a + pltpu.roll(b, 2, axis=0) + bias
    return y[8:]


def _ssd_finish(y, x, dsk, z, nw):
    y = y + dsk * x
    y = y * _silu(z)
    y = y * lax.rsqrt(jnp.mean(jnp.square(y), axis=-1, keepdims=True) + EPS)
    return y * nw


def _ssd_lanes():
    ng = SSD_GROUPS_PER_STEP
    xw, sw = ng * GROUP_WIDTH, ng * D_STATE
    return ((0, xw), (xw, xw + sw), (xw + sw, xw + 2 * sw))


def _ssd_carry_kernel(x_ref, b_ref, c_ref, z_ref, sm_ref, cwx_ref, cwb_ref, cwc_ref, cbx_ref, cbb_ref, cbc_ref,
                      dtb_ref, alog_ref, ex_ref, dsk_ref, nw_ref, csx_ref, csb_ref, csc_ref, h0_ref,
                      ys_ref, h_ref, xpad, *, lead_null):
    rows = x_ref.shape[0]
    nchunk = rows // CHUNK
    rows_p = max(rows, LANE)
    gw = GROUP_WIDTH
    lanes = _ssd_lanes()

    @pl.when(pl.program_id(2) == 0)
    def _():
        h_ref[...] = h0_ref[...]
        for (lo, hi), cs in zip(lanes, (csx_ref, csb_ref, csc_ref)):
            xpad[pl.ds(0, 8), lo:hi] = cs[0]

    for (lo, hi), src in zip(lanes, (x_ref, b_ref, c_ref)):
        xpad[pl.ds(8, rows), lo:hi] = src[...]
    xc = _silu(_conv_rows(xpad, (cwx_ref, cwb_ref, cwc_ref), (cbx_ref, cbb_ref, cbc_ref), rows))
    xpad[pl.ds(0, 8), :] = xpad[pl.ds(rows, 8), :]

    eye = _eye_bf16(gw)
    same, causal, _ = _chunk_masks(rows, rows, lead_null)
    tri, allm = _f01(causal), _f01(same)
    causal_l = _chunk_masks(rows, gw, lead_null, lane_mod=CHUNK - 1)[1]
    row_valid = None
    if lead_null:
        row_valid = (lax.broadcasted_iota(jnp.int32, (rows, 1), 0) & (CHUNK - 1)) >= lead_null
    dt = _softplus(sm_ref[...] + dtb_ref[...])
    dta = dt * (-jnp.exp(alog_ref[...]))
    for gg in range(SSD_GROUPS_PER_STEP):
        xs = slice(gg * gw, (gg + 1) * gw)
        x = xc[:, lanes[0][0] + gg * gw:lanes[0][0] + (gg + 1) * gw]
        bm_b = xc[:, lanes[1][0] + gg * D_STATE:lanes[1][0] + (gg + 1) * D_STATE].astype(BF16)
        cm_b = xc[:, lanes[2][0] + gg * D_STATE:lanes[2][0] + (gg + 1) * D_STATE].astype(BF16)
        decay, xdt, ecb, xw_t, e_last = _ssd_block(x, dt, dta, ex_ref[gg], tri, allm, causal_l, eye, row_valid)
        bpad = _pad_rows(bm_b, rows_p)
        e_rows = [_split3_rows(e_last[c * CHUNK:c * CHUNK + 16], 16) for c in range(nchunk)]
        e_rows.append(jnp.zeros((LANE - 16 * nchunk, gw), BF16))
        e_t = _nt(eye, jnp.concatenate(e_rows, axis=0))
        hs = slice(gg * HEADS_PER_GROUP, (gg + 1) * HEADS_PER_GROUP)
        h = h_ref[0, hs].reshape(gw, D_STATE)
        xw_tb = xw_t.astype(BF16)
        ys = []
        for c in range(nchunk):
            rs = slice(c * CHUNK, (c + 1) * CHUNK)
            y = _ssd_intra(cm_b[rs], bm_b[rs], decay[rs], xdt[rs], CHUNK)
            ys.append(y + _nt(cm_b[rs], h.astype(BF16)) * ecb[rs])
            h = _lane_group_sum(e_t, 4, c) * h + _mm(xw_tb, _chunk_rows_only(bpad, c))
        h_ref[0, hs] = h.reshape(HEADS_PER_GROUP, SSD_HEAD_DIM, D_STATE)
        y = ys[0] if nchunk == 1 else jnp.concatenate(ys, axis=0)
        ys_ref[:, xs] = _ssd_finish(y, x, dsk_ref[:, xs], z_ref[:, xs], nw_ref[:, xs]).astype(ys_ref.dtype)


def _ssd_multi_kernel(x_ref, b_ref, c_ref, z_ref, sm_ref, cwx_ref, cwb_ref, cwc_ref, cbx_ref, cbb_ref, cbc_ref,
                      dtb_ref, alog_ref, ex_ref, dsk_ref, nw_ref, csx_ref, csb_ref, csc_ref, h0_ref,
                      ys_ref, h_ref, xpad, cres, xcs, *, seg_len):
    tb = x_ref.shape[0]
    nseq = tb // seg_len
    seg_shift = seg_len.bit_length() - 1
    gw = GROUP_WIDTH
    lanes = _ssd_lanes()
    xpad[...] = jnp.zeros(xpad.shape, F32)
    for i in range(nseq):
        for (lo, hi), src, cs in zip(lanes, (x_ref, b_ref, c_ref), (csx_ref, csb_ref, csc_ref)):
            xpad[pl.ds(8 + 8 * i + 1, CONV_W - 1), lo:hi] = cs[i]
            xpad[pl.ds(8 + 8 * i + 4, seg_len), lo:hi] = src[pl.ds(seg_len * i, seg_len), :]
    cres[...] = _conv_rows(xpad, (cwx_ref, cwb_ref, cwc_ref), (cbx_ref, cbb_ref, cbc_ref), 8 * nseq)
    for i in range(nseq):
        xcs[pl.ds(seg_len * i, seg_len), :] = cres[pl.ds(8 * i + 4, seg_len), :]
    xc = _silu(xcs[...])

    eye = _eye_bf16(gw)
    same, causal, _ = _seg_masks(tb, tb, seg_len)
    tri, allm = _f01(causal), _f01(same)
    causal_l = _seg_masks(tb, gw, seg_len, lane_mod=CHUNK - 1)[1]
    dt = _softplus(sm_ref[...] + dtb_ref[...])
    dta = dt * (-jnp.exp(alog_ref[...]))
    rowseg = lax.broadcasted_iota(jnp.int32, (tb, 1), 0) >> seg_shift
    for gg in range(SSD_GROUPS_PER_STEP):
        xs = slice(gg * gw, (gg + 1) * gw)
        x = xc[:, lanes[0][0] + gg * gw:lanes[0][0] + (gg + 1) * gw]
        bm_b = xc[:, lanes[1][0] + gg * D_STATE:lanes[1][0] + (gg + 1) * D_STATE].astype(BF16)
        cm_b = xc[:, lanes[2][0] + gg * D_STATE:lanes[2][0] + (gg + 1) * D_STATE].astype(BF16)
        decay, xdt, ecb, xw_t, e_last = _ssd_block(x, dt, dta, ex_ref[gg], tri, allm, causal_l, eye, None)
        bpad = _pad_rows(bm_b, LANE)
        e_rows = jnp.concatenate([_split3_rows(e_last, seg_len), jnp.zeros((LANE - tb, gw), BF16)], axis=0)
        e_t = _nt(eye, e_rows)
        y = _ssd_intra(cm_b, bm_b, decay, xdt, tb)
        hs = slice(gg * HEADS_PER_GROUP, (gg + 1) * HEADS_PER_GROUP)
        for i in range(nseq):
            h = h0_ref[i, hs].reshape(gw, D_STATE)
            y = y + jnp.where(rowseg == i, _nt(cm_b, h.astype(BF16)) * ecb, 0.0)
            h = _lane_group_sum(e_t, seg_shift, i) * h + _mm(_lane_group_keep(xw_t, seg_shift, i), bpad)
            h_ref[i, hs] = h.reshape(HEADS_PER_GROUP, SSD_HEAD_DIM, D_STATE)
        ys_ref[:, xs] = _ssd_finish(y, x, dsk_ref[:, xs], z_ref[:, xs], nw_ref[:, xs]).astype(ys_ref.dtype)


def _ssd_scan(ub, usm, conv_w, conv_b, dtb, alog, expand, dsk, nw, conv0, h0, *,
              n_seq, seq_rows, block_rows, carry, lead_null=0):
    m = ub.shape[0]
    ng = SSD_GROUPS_PER_STEP
    xw, sw = ng * GROUP_WIDTH, ng * D_STATE
    cw = xw + 2 * sw
    hpg = ng * HEADS_PER_GROUP
    if carry:
        nb = seq_rows // block_rows
        grid = (n_seq, SSD_GROUPS // ng, nb)
        c_mul = 1 if conv0.shape[0] == n_seq else 0
        h_mul = 1 if h0.shape[0] == n_seq else 0
        rmap = lambda off: (lambda b, g, c: (b * nb + c, off + g))
        smap = lambda b, g, c: (b * nb + c, 0)
        wmap = lambda off: (lambda b, g, c: (0, off + g))
        zmap = lambda b, g, c: (0, 0)
        exmap = lambda b, g, c: (g, 0, 0)
        cs_spec = lambda width, off: pl.BlockSpec((1, 8, width), lambda b, g, c: (b * c_mul, 0, off + g))
        h0_spec = pl.BlockSpec((1, hpg, SSD_HEAD_DIM, D_STATE), lambda b, g, c: (b * h_mul, g, 0, 0))
        ho_spec = pl.BlockSpec((1, hpg, SSD_HEAD_DIM, D_STATE), lambda b, g, c: (b, g, 0, 0))
        scratch = [pltpu.VMEM((block_rows + 8, cw), F32)]
        sem = ("parallel", "parallel", "arbitrary")
        kern = functools.partial(_ssd_carry_kernel, lead_null=lead_null)
    else:
        per = SEQS_PER_BLOCK
        grid = (m // block_rows, SSD_GROUPS // ng)
        rmap = lambda off: (lambda i, g: (i, off + g))
        smap = lambda i, g: (i, 0)
        wmap = lambda off: (lambda i, g: (0, off + g))
        zmap = lambda i, g: (0, 0)
        exmap = lambda i, g: (g, 0, 0)
        cs_spec = lambda width, off: pl.BlockSpec((per, CONV_W - 1, width), lambda i, g: (i, 0, off + g))
        h0_spec = ho_spec = pl.BlockSpec((per, hpg, SSD_HEAD_DIM, D_STATE), lambda i, g: (i, g, 0, 0))
        scratch = [pltpu.VMEM((8 * per + 8, cw), F32), pltpu.VMEM((8 * per, cw), F32),
                   pltpu.VMEM((block_rows, cw), F32)]
        sem = ("parallel", "parallel")
        kern = functools.partial(_ssd_multi_kernel, seg_len=seq_rows)
    cvb, cvc = SSD_WIDTH // sw, (SSD_WIDTH + BC_WIDTH) // sw
    in_specs = [
        pl.BlockSpec((block_rows, xw), rmap(UB_X // xw)), pl.BlockSpec((block_rows, sw), rmap(UB_B // sw)),
        pl.BlockSpec((block_rows, sw), rmap(UB_C // sw)), pl.BlockSpec((block_rows, xw), rmap(0)),
        pl.BlockSpec((block_rows, LANE), smap),
        pl.BlockSpec((CONV_W, xw), wmap(0)), pl.BlockSpec((CONV_W, sw), wmap(cvb)), pl.BlockSpec((CONV_W, sw), wmap(cvc)),
        pl.BlockSpec((1, xw), wmap(0)), pl.BlockSpec((1, sw), wmap(cvb)), pl.BlockSpec((1, sw), wmap(cvc)),
        pl.BlockSpec((1, LANE), zmap), pl.BlockSpec((1, LANE), zmap),
        pl.BlockSpec((ng, LANE, GROUP_WIDTH), exmap),
        pl.BlockSpec((1, xw), wmap(0)), pl.BlockSpec((1, xw), wmap(0)),
        cs_spec(xw, 0), cs_spec(sw, cvb), cs_spec(sw, cvc),
        h0_spec,
    ]
    out_specs = [pl.BlockSpec((block_rows, xw), rmap(0)), ho_spec]
    return pl.pallas_call(
        kern, grid=grid, in_specs=in_specs, out_specs=out_specs,
        out_shape=[jax.ShapeDtypeStruct((m, SSD_WIDTH), BF16),
                   jax.ShapeDtypeStruct((n_seq, SSD_HEADS, SSD_HEAD_DIM, D_STATE), F32)],
        scratch_shapes=scratch,
        compiler_params=pltpu.CompilerParams(dimension_semantics=sem, vmem_limit_bytes=VMEM_LIMIT),
        name="ssd_scan_carry" if carry else "ssd_scan_multi",
    )(ub, ub, ub, ub, usm, conv_w, conv_w, conv_w, conv_b, conv_b, conv_b, dtb, alog, expand, dsk, nw,
      conv0, conv0, conv0, h0)


def _matmul_kernel(x_ref, w_ref, o_ref, *wb_ref, act, w_t):
    w = w_ref[...].astype(BF16)
    if wb_ref:
        wb_ref[0][...] = w
    acc = _nt(x_ref[...], w) if w_t else _mm(x_ref[...], w)
    if act == "relu2":
        acc = jnp.square(jnp.maximum(acc, 0.0))
    o_ref[...] = acc.astype(o_ref.dtype)


def _matmul(x, w, *, tm, tn, out_dtype, act=None, name, n=None, emit=False, w_t=False, row0=0):
    m, k = x.shape
    n = w.shape[0 if w_t else 1] if n is None else n
    assert not emit or m == tm
    assert row0 == 0 or (w_t and row0 % 16 == 0)
    blk_spec = pl.BlockSpec((tn, k), lambda i, j: (j, 0)) if w_t else pl.BlockSpec((k, tn), lambda i, j: (0, j))
    w_spec = blk_spec if row0 == 0 else pl.BlockSpec((pl.Element(tn), pl.Element(k)),
                                                     lambda i, j: (pl.multiple_of(row0 + j * tn, 16), 0))
    out_specs = [pl.BlockSpec((tm, tn), lambda i, j: (i, j))]
    out_shape = [jax.ShapeDtypeStruct((m, n), out_dtype)]
    if emit:
        out_specs.append(blk_spec)
        out_shape.append(jax.ShapeDtypeStruct((n, k) if w_t else (k, n), BF16))
    res = pl.pallas_call(
        functools.partial(_matmul_kernel, act=act, w_t=w_t),
        grid=(m // tm, n // tn),
        in_specs=[pl.BlockSpec((tm, k), lambda i, j: (i, 0)), w_spec],
        out_specs=out_specs, out_shape=out_shape,
        compiler_params=pltpu.CompilerParams(dimension_semantics=("parallel", "parallel"),
                                             vmem_limit_bytes=VMEM_LIMIT),
        name=name,
    )(x, w)
    return res if emit else res[0]


def _out_proj_kernel(a_ref, b_ref, wa_ref, wb_ref, res_ref, o_ref, *wo_refs):
    wa, wb = wa_ref[...], wb_ref[...]
    if wo_refs:
        wa, wb = wa.astype(BF16), wb.astype(BF16)
        wo_refs[0][...] = wa
        wo_refs[1][...] = wb
    o_ref[...] = ALPHA * res_ref[...] + _mm(a_ref[...], wa) + _mm(b_ref[...], wb)


def _out_proj(a, b, w_top, w_bot, res, *, tm, tn, emit=False):
    m, kh = a.shape
    n = w_top.shape[1]
    assert not emit or m == tm
    out_specs = [pl.BlockSpec((tm, tn), lambda i, j: (i, j))]
    out_shape = [jax.ShapeDtypeStruct((m, n), F32)]
    if emit:
        out_specs += [pl.BlockSpec((kh, tn), lambda i, j: (0, j))] * 2
        out_shape += [jax.ShapeDtypeStruct((kh, n), BF16)] * 2
    res_ = pl.pallas_call(
        _out_proj_kernel,
        grid=(m // tm, n // tn),
        in_specs=[pl.BlockSpec((tm, kh), lambda i, j: (i, 0)), pl.BlockSpec((tm, kh), lambda i, j: (i, 0)),
                  pl.BlockSpec((kh, tn), lambda i, j: (0, j)),
                  pl.BlockSpec((kh, tn), lambda i, j: (1 if emit else 0, j)),
                  pl.BlockSpec((tm, tn), lambda i, j: (i, j))],
        out_specs=out_specs, out_shape=out_shape,
        compiler_params=pltpu.CompilerParams(dimension_semantics=("parallel", "parallel"),
                                             vmem_limit_bytes=VMEM_LIMIT),
        name="out_proj",
    )(a, b, w_top, w_bot, res)
    return res_ if emit else res_[0]


def _down_proj_kernel(x_ref, w_ref, res_ref, o_ref, *rest):
    acc_ref = rest[-1]
    kk = pl.program_id(2)

    @pl.when(kk == 0)
    def _():
        acc_ref[...] = ALPHA * res_ref[...]

    w = w_ref[...]
    if len(rest) == 2:
        w = w.astype(BF16)
        rest[0][...] = w
    acc_ref[...] += _mm(x_ref[...], w)

    @pl.when(kk == pl.num_programs(2) - 1)
    def _():
        o_ref[...] = acc_ref[...]


def _down_proj(x, w, res, *, tm, tn, tk, emit=False):
    m, k = x.shape
    n = w.shape[1]
    assert not emit or m == tm
    out_specs = [pl.BlockSpec((tm, tn), lambda i, j, kk: (i, j))]
    out_shape = [jax.ShapeDtypeStruct((m, n), F32)]
    if emit:
        out_specs.append(pl.BlockSpec((tk, tn), lambda i, j, kk: (kk, j)))
        out_shape.append(jax.ShapeDtypeStruct((k, n), BF16))
    res_ = pl.pallas_call(
        _down_proj_kernel,
        grid=(m // tm, n // tn, k // tk),
        in_specs=[pl.BlockSpec((tm, tk), lambda i, j, kk: (i, kk)), pl.BlockSpec((tk, tn), lambda i, j, kk: (kk, j)),
                  pl.BlockSpec((tm, tn), lambda i, j, kk: (i, j))],
        out_specs=out_specs, out_shape=out_shape,
        scratch_shapes=[pltpu.VMEM((tm, tn), F32)],
        compiler_params=pltpu.CompilerParams(dimension_semantics=("parallel", "parallel", "arbitrary"),
                                             vmem_limit_bytes=VMEM_LIMIT),
        name="down_proj",
    )(x, w, res)
    return res_ if emit else res_[0]


def _layer_norm_kernel(x_ref, g_ref, b_ref, *o_refs):
    x = x_ref[...]
    mu = jnp.mean(x, axis=-1, keepdims=True)
    xc = x - mu
    var = jnp.mean(jnp.square(xc), axis=-1, keepdims=True)
    y = xc * lax.rsqrt(var + EPS) * g_ref[...] + b_ref[...]
    for o_ref in o_refs:
        o_ref[...] = y.astype(o_ref.dtype)


def _layer_norm(x, g, b, *, tm, dtypes):
    m, d = x.shape
    return pl.pallas_call(
        _layer_norm_kernel,
        grid=(m // tm,),
        in_specs=[pl.BlockSpec((tm, d), lambda i: (i, 0)), pl.BlockSpec((1, d), lambda i: (0, 0)),
                  pl.BlockSpec((1, d), lambda i: (0, 0))],
        out_specs=[pl.BlockSpec((tm, d), lambda i: (i, 0)) for _ in dtypes],
        out_shape=[jax.ShapeDtypeStruct((m, d), dt) for dt in dtypes],
        compiler_params=pltpu.CompilerParams(dimension_semantics=("parallel",), vmem_limit_bytes=VMEM_LIMIT),
        name="layer_norm",
    )(x, g, b)


def _expand_matrix():
    e = np.zeros((SSD_GROUPS, LANE, GROUP_WIDTH), np.float32)
    for g in range(SSD_GROUPS):
        for j in range(HEADS_PER_GROUP):
            e[g, SM_DT + g * HEADS_PER_GROUP + j, j * SSD_HEAD_DIM:(j + 1) * SSD_HEAD_DIM] = 1.0
    return jnp.asarray(e, BF16)


def _lane_param(p):
    return jnp.zeros((1, LANE), F32).at[0, SM_DT:SM_DT + SSD_HEADS].set(p.astype(F32))


def _dense_tail(x, og, ys, prm, tm, emit):
    wb = dict(prm)
    tn = 512 if emit else 1024
    w_top, w_bot = (prm["w_out"], prm["w_out"]) if emit else prm["w_out"]
    pre1 = _out_proj(og, ys, w_top, w_bot, x, tm=tm, tn=tn, emit=emit)
    if emit:
        pre1, wt, wbt = pre1
        wb["w_out"] = (wt, wbt)
    h, hb = _layer_norm(pre1, prm["ln1_g"], prm["ln1_b"], tm=min(tm, 256), dtypes=(F32, BF16))
    hid = _matmul(hb, prm["w_up"], tm=tm, tn=tn, out_dtype=BF16, act="relu2", name="mlp_up", emit=emit)
    if emit:
        hid, wb["w_up"] = hid
    pre2 = _down_proj(hid, prm["w_down"], h, tm=tm, tn=1024, tk=2048, emit=emit)
    if emit:
        pre2, wb["w_down"] = pre2
    (out,) = _layer_norm(pre2, prm["ln2_g"], prm["ln2_b"], tm=min(tm, 256), dtypes=(F32,))
    return out, wb


def kernel(x_prompt, x_sample, state_gla, state_ssm, state_conv, meta_tokens, w_in, w_gk_up, b_gk, gla_norm_w,
           conv_w, conv_b, dt_bias, a_log, d_skip, ssd_norm_w, w_out, ln1_g, ln1_b, w_up, w_down, ln2_g, ln2_b):
    bp, seq, d = x_prompt.shape
    bs, dec = x_sample.shape[:2]
    assert w_in.shape[0] == DEPTH and seq % CHUNK == 0 and dec == 4 and N_META <= CHUNK
    assert bs % SEQS_PER_BLOCK == 0
    wi = w_in[0]
    o_g = UA_WIDTH
    o_z = o_g + GATE_RANK
    o_dt = o_z + UB_WIDTH
    wt = wi.T
    w_small = jnp.concatenate([wt[o_g:o_z], wt[o_dt:], jnp.zeros((LANE - GATE_RANK - SSD_HEADS, d), F32)], axis=0)
    wg = jnp.concatenate([w_gk_up[0], jnp.zeros((LANE - GATE_RANK, GLA_QK), F32)], axis=0).astype(BF16)
    prm = dict(
        w_out=w_out[0], w_up=w_up[0], w_down=w_down[0],
        ln1_g=ln1_g[0][None], ln1_b=ln1_b[0][None], ln2_g=ln2_g[0][None], ln2_b=ln2_b[0][None],
    )
    gla_args = (wg, b_gk[0][None], gla_norm_w[0][None])
    ssd_args = (conv_w[0], conv_b[0][None], _lane_param(dt_bias[0]), _lane_param(a_log[0]), _expand_matrix(),
                jnp.repeat(d_skip[0], SSD_HEAD_DIM)[None, :], ssd_norm_w[0][None])
    tail = CONV_W - 1

    def in_proj(xb, tm, w_ab):
        if w_ab is None:
            ua, w_a = _matmul(xb, wt, tm=tm, tn=512, out_dtype=F32, name="in_proj_gla", n=UA_WIDTH, emit=True,
                              w_t=True)
            ub, w_b = _matmul(xb, wt, tm=tm, tn=512, out_dtype=F32, name="in_proj_ssd", n=UB_WIDTH, emit=True,
                              w_t=True, row0=o_z)
        else:
            w_a, w_b = w_ab
            ua = _matmul(xb, w_a, tm=tm, tn=1024, out_dtype=F32, name="in_proj_gla", w_t=True)
            ub = _matmul(xb, w_b, tm=tm, tn=1024, out_dtype=F32, name="in_proj_ssd", w_t=True)
        usm = _matmul(xb, w_small, tm=tm, tn=LANE, out_dtype=F32, name="in_proj_small", w_t=True)
        return ua, ub, usm, (w_a, w_b)

    xs = x_sample.reshape(bs * dec, d)
    ua_s, ub_s, usm_s, w_ab = in_proj(xs.astype(BF16), bs * dec, None)
    blk = SEQS_PER_BLOCK * dec
    og_s, gla_s = _gla_scan(ua_s, usm_s, *gla_args, state_gla[0], n_seq=bs, seq_rows=dec, block_rows=blk,
                            carry=False)
    ys_s, ssm_s = _ssd_scan(ub_s, usm_s, *ssd_args, state_conv[0], state_ssm[0], n_seq=bs, seq_rows=dec,
                            block_rows=blk, carry=False)
    y_sample, prm_b = _dense_tail(xs, og_s, ys_s, prm, bs * dec, True)
    y_sample = y_sample.reshape(bs, dec, d)
    conv_s = ub_s[:, UB_X:].reshape(bs, dec, CONV_DIM)[:, dec - tail:]

    lead_null = CHUNK - N_META
    xm = jnp.concatenate([jnp.zeros((lead_null, d), F32), meta_tokens], axis=0).astype(BF16)
    ua_m, ub_m, usm_m, _ = in_proj(xm, CHUNK, w_ab)
    _, gla_m = _gla_scan(ua_m, usm_m, *gla_args, jnp.zeros((1, GLA_HEADS, GLA_DK, GLA_DV), F32),
                         n_seq=1, seq_rows=CHUNK, block_rows=CHUNK, carry=True, lead_null=lead_null)
    _, ssm_m = _ssd_scan(ub_m, usm_m, *ssd_args, jnp.zeros((1, 8, CONV_DIM), F32),
                         jnp.zeros((1, SSD_HEADS, SSD_HEAD_DIM, D_STATE), F32),
                         n_seq=1, seq_rows=CHUNK, block_rows=CHUNK, carry=True, lead_null=lead_null)
    conv_m = ub_m[None, CHUNK - 8:, UB_X:]

    xp = x_prompt.reshape(bp * seq, d)
    ua_p, ub_p, usm_p, _ = in_proj(xp.astype(BF16), 1024, w_ab)
    og_p, gla_p = _gla_scan(ua_p, usm_p, *gla_args, gla_m, n_seq=bp, seq_rows=seq, block_rows=256, carry=True)
    ys_p, ssm_p = _ssd_scan(ub_p, usm_p, *ssd_args, conv_m, ssm_m, n_seq=bp, seq_rows=seq, block_rows=256,
                            carry=True)
    y_prompt, _ = _dense_tail(xp, og_p, ys_p, prm_b, 1024, False)
    y_prompt = y_prompt.reshape(bp, seq, d)
    conv_p = jnp.stack([ub_p[(b + 1) * seq - tail:(b + 1) * seq, UB_X:] for b in range(bp)])

    return (y_prompt, y_sample, gla_p[None], ssm_p[None], conv_p[None], gla_s[None], ssm_s[None], conv_s[None])
```

```python
import functools

import numpy as np
import jax
import jax.numpy as jnp
from jax import lax
from jax.experimental import pallas as pl
from jax.experimental.pallas import tpu as pltpu

F32 = jnp.float32
BF16 = jnp.bfloat16

GLA_HEADS = 4
GLA_DK = 256
GLA_DV = 512
GLA_QK = GLA_HEADS * GLA_DK
GLA_WIDTH = GLA_HEADS * GLA_DV
GATE_RANK = 16
GATE_NORM = 16.0
SSD_GROUPS = 8
HEADS_PER_GROUP = 4
SSD_HEAD_DIM = 64
SSD_HEADS = SSD_GROUPS * HEADS_PER_GROUP
GROUP_WIDTH = HEADS_PER_GROUP * SSD_HEAD_DIM
SSD_WIDTH = SSD_GROUPS * GROUP_WIDTH
D_STATE = 128
BC_WIDTH = SSD_GROUPS * D_STATE
CONV_W = 4
CONV_DIM = SSD_WIDTH + 2 * BC_WIDTH
N_META = 16
CHUNK = 64
CHUNK_SHIFT = 6
EPS = 1e-5
DEPTH = 1
ALPHA = (2.0 * DEPTH) ** 0.25

UA_K = GLA_QK
UA_V = 2 * GLA_QK
UA_R = UA_V + GLA_WIDTH
UA_WIDTH = UA_R + GLA_WIDTH
UB_X = SSD_WIDTH
UB_B = UB_X + SSD_WIDTH
UB_C = UB_B + BC_WIDTH
UB_WIDTH = UB_C + BC_WIDTH
LANE = 128
SM_DT = GATE_RANK

GLA_HEADS_PER_STEP = 4
SSD_GROUPS_PER_STEP = 4
SEQS_PER_BLOCK = 4

VMEM_LIMIT = 56 * 1024 * 1024


def _mm(a, b):
    return jnp.dot(a, b, preferred_element_type=F32)


def _nt(a, b):
    return lax.dot_general(a, b, (((1,), (1,)), ((), ())), preferred_element_type=F32)


def _split2(x):
    hi = x.astype(BF16)
    lo = (x - hi.astype(F32)).astype(BF16)
    return hi, lo


def _mm_split(m01, x):
    hi, lo = _split2(x)
    return _mm(m01, hi) + _mm(m01, lo)


def _pad_rows(x, n):
    t = x.shape[0]
    if t >= n:
        return x
    return jnp.concatenate([x, jnp.zeros((n - t, x.shape[1]), x.dtype)], axis=0)


def _silu(x):
    return x * (1.0 / (1.0 + jnp.exp(-x)))


def _softplus(x):
    return jnp.maximum(x, 0.0) + jnp.log(1.0 + jnp.exp(-jnp.abs(x)))


def _f01(m):
    return jnp.where(m, 1.0, 0.0).astype(BF16)


def _eye_bf16(n):
    ri = lax.broadcasted_iota(jnp.int32, (n, n), 0)
    ci = lax.broadcasted_iota(jnp.int32, (n, n), 1)
    return _f01(ri == ci)


def _split3_rows(x, period):
    hi = x.astype(BF16)
    r1 = x - hi.astype(F32)
    mid = r1.astype(BF16)
    lo = (r1 - mid.astype(F32)).astype(BF16)
    ridx = lax.broadcasted_iota(jnp.int32, x.shape, 0) & (period - 1)
    out = jnp.where(ridx == 0, hi.astype(F32),
                    jnp.where(ridx == 1, mid.astype(F32),
                              jnp.where(ridx == 2, lo.astype(F32), 0.0)))
    return out.astype(BF16)


def _lane_group_sum(t, shift, idx):
    lane = lax.broadcasted_iota(jnp.int32, t.shape, 1)
    return jnp.sum(jnp.where((lane >> shift) == idx, t, 0.0), axis=1, keepdims=True)


def _chunk_rows_only(xb, c):
    rows_p, w = xb.shape
    lo, hi = c * CHUNK, (c + 1) * CHUNK
    pieces = []
    if lo:
        pieces.append(jnp.zeros((lo, w), xb.dtype))
    pieces.append(xb[lo:hi])
    if hi < rows_p:
        pieces.append(jnp.zeros((rows_p - hi, w), xb.dtype))
    return pieces[0] if len(pieces) == 1 else jnp.concatenate(pieces, axis=0)


def _lane_group_keep(t, shift, idx):
    lane = lax.broadcasted_iota(jnp.int32, t.shape, 1)
    return jnp.where((lane >> shift) == idx, t, 0.0).astype(BF16)


def _chunk_masks(rows, cols, lead_null, lane_mod=None):
    ri = lax.broadcasted_iota(jnp.int32, (rows, cols), 0)
    ci = lax.broadcasted_iota(jnp.int32, (rows, cols), 1)
    if lane_mod is None:
        same = (ri >> CHUNK_SHIFT) == (ci >> CHUNK_SHIFT)
        cl = ci & (CHUNK - 1)
    else:
        cl = ci & lane_mod
        same = None
    rl = ri & (CHUNK - 1)
    valid = cl >= lead_null if lead_null else None
    base = same
    if valid is not None:
        base = valid if base is None else (base & valid)
    causal = (cl <= rl) if base is None else (base & (cl <= rl))
    middle = (lead_null + CHUNK) // 2 - 1
    upto_mid = (cl <= middle) if base is None else (base & (cl <= middle))
    return base, causal, upto_mid


def _seg_masks(tb, cols, seg_len, lane_mod=None):
    ri = lax.broadcasted_iota(jnp.int32, (tb, cols), 0)
    ci = lax.broadcasted_iota(jnp.int32, (tb, cols), 1)
    if lane_mod is not None:
        ci = ci & lane_mod
    shift = seg_len.bit_length() - 1
    same = ((ri >> shift) == (ci >> shift)) & (ci < tb)
    causal = same & (ci <= ri)
    upto_mid = same & ((ci & (seg_len - 1)) <= seg_len // 2 - 1)
    return same, causal, upto_mid


def _gla_gate(sm_b, wg, bg):
    gpre = _mm(sm_b, wg) + bg
    return -_softplus(-gpre) * (1.0 / GATE_NORM)


def _gla_finish(o, gnw, r):
    o = o * lax.rsqrt(jnp.mean(jnp.square(o), axis=-1, keepdims=True) + EPS)
    return o * gnw * _silu(r)


def _gla_block(q, k, v, g, mats, causal_p, eye):
    tri, midm, allm = mats
    rows, dk = q.shape
    rows_p = max(rows, LANE)
    bc = _mm_split(tri, g)
    anchor = _mm_split(midm, g)
    blast = _mm_split(allm, g)
    qs = q * (dk ** -0.5)
    qt = (qs * jnp.exp(bc - anchor)).astype(BF16)
    kt = _pad_rows((k * jnp.exp(anchor - bc)).astype(BF16), rows_p)
    p = jnp.where(causal_p, _nt(qt, kt), 0.0)
    vb = _pad_rows(v.astype(BF16), rows_p)
    o = _mm(p.astype(BF16), vb)
    qb = (qs * jnp.exp(bc)).astype(BF16)
    kd = _pad_rows((k * jnp.exp(blast - bc)).astype(BF16), rows_p)
    kd_t = _nt(eye, kd)
    return o, qb, kd_t, vb, jnp.exp(blast)


def _gla_carry_kernel(q_ref, k_ref, v_ref, r_ref, sm_ref, wg_ref, bg_ref, gnw_ref, s0_ref,
                      og_ref, s_ref, *, lead_null):
    rows = q_ref.shape[0]
    nchunk = rows // CHUNK
    rows_p = max(rows, LANE)

    @pl.when(pl.program_id(2) == 0)
    def _():
        s_ref[...] = s0_ref[...]

    eye = _eye_bf16(GLA_DK)
    same, causal, upto_mid = _chunk_masks(rows, rows, lead_null)
    mats = (_f01(causal), _f01(upto_mid), _f01(same))
    causal_p = _chunk_masks(rows, rows_p, lead_null)[1]
    sm_b = sm_ref[...].astype(BF16)
    for hh in range(GLA_HEADS_PER_STEP):
        ks = slice(hh * GLA_DK, (hh + 1) * GLA_DK)
        vs = slice(hh * GLA_DV, (hh + 1) * GLA_DV)
        g = _gla_gate(sm_b, wg_ref[:, ks], bg_ref[:, ks])
        k = k_ref[:, ks]
        if lead_null:
            rl = lax.broadcasted_iota(jnp.int32, (rows, 1), 0) & (CHUNK - 1)
            k = jnp.where(rl >= lead_null, k, 0.0)
        o, qb, kd_t, vb, e_last = _gla_block(q_ref[:, ks], k, v_ref[:, vs], g, mats, causal_p, eye)
        e_rows = [_split3_rows(e_last[c * CHUNK:c * CHUNK + 16], 16) for c in range(nchunk)]
        e_rows.append(jnp.zeros((LANE - 16 * nchunk, GLA_DK), BF16))
        e_t = _nt(eye, jnp.concatenate(e_rows, axis=0))
        s = s_ref[0, hh]
        kd_tb = kd_t.astype(BF16)
        inter = []
        for c in range(nchunk):
            inter.append(_mm(qb[c * CHUNK:(c + 1) * CHUNK], s.astype(BF16)))
            s = _lane_group_sum(e_t, 4, c) * s + _mm(kd_tb, _chunk_rows_only(vb, c))
        s_ref[0, hh] = s
        o = o + (inter[0] if nchunk == 1 else jnp.concatenate(inter, axis=0))
        og_ref[:, vs] = _gla_finish(o, gnw_ref[...], r_ref[:, vs]).astype(og_ref.dtype)


def _gla_multi_kernel(q_ref, k_ref, v_ref, r_ref, sm_ref, wg_ref, bg_ref, gnw_ref, s0_ref,
                      og_ref, s_ref, *, seg_len):
    tb = q_ref.shape[0]
    nseq = tb // seg_len
    seg_shift = seg_len.bit_length() - 1
    eye = _eye_bf16(GLA_DK)
    same, causal, upto_mid = _seg_masks(tb, tb, seg_len)
    mats = (_f01(causal), _f01(upto_mid), _f01(same))
    causal_p = _seg_masks(tb, LANE, seg_len)[1]
    sm_b = sm_ref[...].astype(BF16)
    for hh in range(GLA_HEADS_PER_STEP):
        ks = slice(hh * GLA_DK, (hh + 1) * GLA_DK)
        vs = slice(hh * GLA_DV, (hh + 1) * GLA_DV)
        g = _gla_gate(sm_b, wg_ref[:, ks], bg_ref[:, ks])
        o, qb, kd_t, vb, e_last = _gla_block(q_ref[:, ks], k_ref[:, ks], v_ref[:, vs], g, mats, causal_p, eye)
        e_rows = jnp.concatenate([_split3_rows(e_last, seg_len), jnp.zeros((LANE - tb, GLA_DK), BF16)], axis=0)
        e_t = _nt(eye, e_rows)
        rowseg = lax.broadcasted_iota(jnp.int32, (tb, 1), 0) >> seg_shift
        for i in range(nseq):
            s = s0_ref[i, hh]
            o = o + jnp.where(rowseg == i, _mm(qb, s.astype(BF16)), 0.0)
            s_ref[i, hh] = (_lane_group_sum(e_t, seg_shift, i) * s
                            + _mm(_lane_group_keep(kd_t, seg_shift, i), vb))
        og_ref[:, vs] = _gla_finish(o, gnw_ref[...], r_ref[:, vs]).astype(og_ref.dtype)


def _gla_scan(ua, usm, wg, bg, gnw, s0, *, n_seq, seq_rows, block_rows, carry, lead_null=0):
    m = ua.shape[0]
    nh = GLA_HEADS_PER_STEP
    kw, vw = nh * GLA_DK, nh * GLA_DV
    if carry:
        nb = seq_rows // block_rows
        grid = (n_seq, GLA_HEADS // nh, nb)
        s0_mul = 1 if s0.shape[0] == n_seq else 0
        rmap = lambda off: (lambda b, h, c: (b * nb + c, off + h))
        smap = lambda b, h, c: (b * nb + c, 0)
        cmap = lambda b, h, c: (0, h)
        zmap = lambda b, h, c: (0, 0)
        s0_spec = pl.BlockSpec((1, nh, GLA_DK, GLA_DV), lambda b, h, c: (b * s0_mul, h, 0, 0))
        so_spec = pl.BlockSpec((1, nh, GLA_DK, GLA_DV), lambda b, h, c: (b, h, 0, 0))
        sem = ("parallel", "parallel", "arbitrary")
        kern = functools.partial(_gla_carry_kernel, lead_null=lead_null)
    else:
        grid = (m // block_rows, GLA_HEADS // nh)
        rmap = lambda off: (lambda i, h: (i, off + h))
        smap = lambda i, h: (i, 0)
        cmap = lambda i, h: (0, h)
        zmap = lambda i, h: (0, 0)
        s0_spec = so_spec = pl.BlockSpec((SEQS_PER_BLOCK, nh, GLA_DK, GLA_DV), lambda i, h: (i, h, 0, 0))
        sem = ("parallel", "parallel")
        kern = functools.partial(_gla_multi_kernel, seg_len=seq_rows)
    in_specs = [
        pl.BlockSpec((block_rows, kw), rmap(0)), pl.BlockSpec((block_rows, kw), rmap(UA_K // kw)),
        pl.BlockSpec((block_rows, vw), rmap(UA_V // vw)), pl.BlockSpec((block_rows, vw), rmap(UA_R // vw)),
        pl.BlockSpec((block_rows, LANE), smap),
        pl.BlockSpec((LANE, kw), cmap), pl.BlockSpec((1, kw), cmap), pl.BlockSpec((1, GLA_DV), zmap),
        s0_spec,
    ]
    out_specs = [pl.BlockSpec((block_rows, vw), rmap(0)), so_spec]
    return pl.pallas_call(
        kern, grid=grid, in_specs=in_specs, out_specs=out_specs,
        out_shape=[jax.ShapeDtypeStruct((m, GLA_WIDTH), BF16),
                   jax.ShapeDtypeStruct((n_seq, GLA_HEADS, GLA_DK, GLA_DV), F32)],
        compiler_params=pltpu.CompilerParams(dimension_semantics=sem, vmem_limit_bytes=VMEM_LIMIT),
        name="gla_scan_carry" if carry else "gla_scan_multi",
    )(ua, ua, ua, ua, usm, wg, bg, gnw, s0)


def _ssd_block(x, dt, dta, expand, tri, allm, causal_l, eye, row_valid):
    rows, gw = x.shape
    rows_p = max(rows, LANE)
    p_hi, p_lo = [_mm(p, expand).astype(BF16) for p in _split2(dta)]
    cb = _mm(tri, p_hi) + _mm(tri, p_lo)
    cl = _mm(allm, p_hi) + _mm(allm, p_lo)
    ri = lax.broadcasted_iota(jnp.int32, (rows, gw), 0)
    li = lax.broadcasted_iota(jnp.int32, (rows, gw), 1)
    diag = jnp.where((li & (CHUNK - 1)) == (ri & (CHUNK - 1)), cb, 0.0)
    rw = _mm_split(allm, diag)
    decay = jnp.exp(jnp.where(causal_l, cb - rw, -jnp.inf))
    dt_hi, dt_lo = _split2(dt)
    xdt = x * (_mm(dt_hi, expand) + _mm(dt_lo, expand))
    if row_valid is not None:
        xdt = jnp.where(row_valid, xdt, 0.0)
    xw_t = _nt(eye, _pad_rows((xdt * jnp.exp(cl - cb)).astype(BF16), rows_p))
    return decay, xdt, jnp.exp(cb), xw_t, jnp.exp(cl)


def _ssd_intra(cm_b, bm_b, decay, xdt, seg_rows):
    gw = xdt.shape[1]
    b64 = _pad_rows(bm_b, CHUNK)
    cb4 = _nt(cm_b, jnp.concatenate([b64] * HEADS_PER_GROUP, axis=0))
    x64 = _pad_rows(xdt, CHUNK)
    r2 = lax.broadcasted_iota(jnp.int32, (gw, gw), 0)
    l2 = lax.broadcasted_iota(jnp.int32, (gw, gw), 1)
    bd = jnp.where((r2 >> CHUNK_SHIFT) == (l2 >> CHUNK_SHIFT),
                   jnp.concatenate([x64] * HEADS_PER_GROUP, axis=0), 0.0)
    return _mm((cb4 * decay).astype(BF16), bd.astype(BF16))


def _conv_rows(xpad, cw_refs, cb_refs, n):
    w = [jnp.concatenate([r[pl.ds(i, 1), :] for r in cw_refs], axis=1) for i in range(CONV_W)]
    bias = jnp.concatenate([r[...] for r in cb_refs], axis=1)
    xa = xpad[pl.ds(0, n + 8), :]
    x1 = pltpu.roll(xa, 1, axis=0)
    a = w[3] * xa + w[2] * x1
    b = w[1] * xa + w[0] * x1
    y = a + pltpu.roll(b, 2, axis=0) + bias
    return y[8:]


def _ssd_finish(y, x, dsk, z, nw):
    y = y + dsk * x
    y = y * _silu(z)
    y = y * lax.rsqrt(jnp.mean(jnp.square(y), axis=-1, keepdims=True) + EPS)
    return y * nw


def _ssd_lanes():
    ng = SSD_GROUPS_PER_STEP
    xw, sw = ng * GROUP_WIDTH, ng * D_STATE
    return ((0, xw), (xw, xw + sw), (xw + sw, xw + 2 * sw))


def _ssd_carry_kernel(x_ref, b_ref, c_ref, z_ref, sm_ref, cwx_ref, cwb_ref, cwc_ref, cbx_ref, cbb_ref, cbc_ref,
                      dtb_ref, alog_ref, ex_ref, dsk_ref, nw_ref, csx_ref, csb_ref, csc_ref, h0_ref,
                      ys_ref, h_ref, xpad, *, lead_null):
    rows = x_ref.shape[0]
    nchunk = rows // CHUNK
    rows_p = max(rows, LANE)
    gw = GROUP_WIDTH
    lanes = _ssd_lanes()

    @pl.when(pl.program_id(2) == 0)
    def _():
        h_ref[...] = h0_ref[...]
        for (lo, hi), cs in zip(lanes, (csx_ref, csb_ref, csc_ref)):
            xpad[pl.ds(0, 8), lo:hi] = cs[0]

    for (lo, hi), src in zip(lanes, (x_ref, b_ref, c_ref)):
        xpad[pl.ds(8, rows), lo:hi] = src[...]
    xc = _silu(_conv_rows(xpad, (cwx_ref, cwb_ref, cwc_ref), (cbx_ref, cbb_ref, cbc_ref), rows))
    xpad[pl.ds(0, 8), :] = xpad[pl.ds(rows, 8), :]

    eye = _eye_bf16(gw)
    same, causal, _ = _chunk_masks(rows, rows, lead_null)
    tri, allm = _f01(causal), _f01(same)
    causal_l = _chunk_masks(rows, gw, lead_null, lane_mod=CHUNK - 1)[1]
    row_valid = None
    if lead_null:
        row_valid = (lax.broadcasted_iota(jnp.int32, (rows, 1), 0) & (CHUNK - 1)) >= lead_null
    dt = _softplus(sm_ref[...] + dtb_ref[...])
    dta = dt * (-jnp.exp(alog_ref[...]))
    for gg in range(SSD_GROUPS_PER_STEP):
        xs = slice(gg * gw, (gg + 1) * gw)
        x = xc[:, lanes[0][0] + gg * gw:lanes[0][0] + (gg + 1) * gw]
        bm_b = xc[:, lanes[1][0] + gg * D_STATE:lanes[1][0] + (gg + 1) * D_STATE].astype(BF16)
        cm_b = xc[:, lanes[2][0] + gg * D_STATE:lanes[2][0] + (gg + 1) * D_STATE].astype(BF16)
        decay, xdt, ecb, xw_t, e_last = _ssd_block(x, dt, dta, ex_ref[gg], tri, allm, causal_l, eye, row_valid)
        bpad = _pad_rows(bm_b, rows_p)
        e_rows = [_split3_rows(e_last[c * CHUNK:c * CHUNK + 16], 16) for c in range(nchunk)]
        e_rows.append(jnp.zeros((LANE - 16 * nchunk, gw), BF16))
        e_t = _nt(eye, jnp.concatenate(e_rows, axis=0))
        hs = slice(gg * HEADS_PER_GROUP, (gg + 1) * HEADS_PER_GROUP)
        h = h_ref[0, hs].reshape(gw, D_STATE)
        xw_tb = xw_t.astype(BF16)
        ys = []
        for c in range(nchunk):
            rs = slice(c * CHUNK, (c + 1) * CHUNK)
            y = _ssd_intra(cm_b[rs], bm_b[rs], decay[rs], xdt[rs], CHUNK)
            ys.append(y + _nt(cm_b[rs], h.astype(BF16)) * ecb[rs])
            h = _lane_group_sum(e_t, 4, c) * h + _mm(xw_tb, _chunk_rows_only(bpad, c))
        h_ref[0, hs] = h.reshape(HEADS_PER_GROUP, SSD_HEAD_DIM, D_STATE)
        y = ys[0] if nchunk == 1 else jnp.concatenate(ys, axis=0)
        ys_ref[:, xs] = _ssd_finish(y, x, dsk_ref[:, xs], z_ref[:, xs], nw_ref[:, xs]).astype(ys_ref.dtype)


def _ssd_multi_kernel(x_ref, b_ref, c_ref, z_ref, sm_ref, cwx_ref, cwb_ref, cwc_ref, cbx_ref, cbb_ref, cbc_ref,
                      dtb_ref, alog_ref, ex_ref, dsk_ref, nw_ref, csx_ref, csb_ref, csc_ref, h0_ref,
                      ys_ref, h_ref, xpad, cres, xcs, *, seg_len):
    tb = x_ref.shape[0]
    nseq = tb // seg_len
    seg_shift = seg_len.bit_length() - 1
    gw = GROUP_WIDTH
    lanes = _ssd_lanes()
    xpad[...] = jnp.zeros(xpad.shape, F32)
    for i in range(nseq):
        for (lo, hi), src, cs in zip(lanes, (x_ref, b_ref, c_ref), (csx_ref, csb_ref, csc_ref)):
            xpad[pl.ds(8 + 8 * i + 1, CONV_W - 1), lo:hi] = cs[i]
            xpad[pl.ds(8 + 8 * i + 4, seg_len), lo:hi] = src[pl.ds(seg_len * i, seg_len), :]
    cres[...] = _conv_rows(xpad, (cwx_ref, cwb_ref, cwc_ref), (cbx_ref, cbb_ref, cbc_ref), 8 * nseq)
    for i in range(nseq):
        xcs[pl.ds(seg_len * i, seg_len), :] = cres[pl.ds(8 * i + 4, seg_len), :]
    xc = _silu(xcs[...])

    eye = _eye_bf16(gw)
    same, causal, _ = _seg_masks(tb, tb, seg_len)
    tri, allm = _f01(causal), _f01(same)
    causal_l = _seg_masks(tb, gw, seg_len, lane_mod=CHUNK - 1)[1]
    dt = _softplus(sm_ref[...] + dtb_ref[...])
    dta = dt * (-jnp.exp(alog_ref[...]))
    rowseg = lax.broadcasted_iota(jnp.int32, (tb, 1), 0) >> seg_shift
    for gg in range(SSD_GROUPS_PER_STEP):
        xs = slice(gg * gw, (gg + 1) * gw)
        x = xc[:, lanes[0][0] + gg * gw:lanes[0][0] + (gg + 1) * gw]
        bm_b = xc[:, lanes[1][0] + gg * D_STATE:lanes[1][0] + (gg + 1) * D_STATE].astype(BF16)
        cm_b = xc[:, lanes[2][0] + gg * D_STATE:lanes[2][0] + (gg + 1) * D_STATE].astype(BF16)
        decay, xdt, ecb, xw_t, e_last = _ssd_block(x, dt, dta, ex_ref[gg], tri, allm, causal_l, eye, None)
        bpad = _pad_rows(bm_b, LANE)
        e_rows = jnp.concatenate([_split3_rows(e_last, seg_len), jnp.zeros((LANE - tb, gw), BF16)], axis=0)
        e_t = _nt(eye, e_rows)
        y = _ssd_intra(cm_b, bm_b, decay, xdt, tb)
        hs = slice(gg * HEADS_PER_GROUP, (gg + 1) * HEADS_PER_GROUP)
        for i in range(nseq):
            h = h0_ref[i, hs].reshape(gw, D_STATE)
            y = y + jnp.where(rowseg == i, _nt(cm_b, h.astype(BF16)) * ecb, 0.0)
            h = _lane_group_sum(e_t, seg_shift, i) * h + _mm(_lane_group_keep(xw_t, seg_shift, i), bpad)
            h_ref[i, hs] = h.reshape(HEADS_PER_GROUP, SSD_HEAD_DIM, D_STATE)
        ys_ref[:, xs] = _ssd_finish(y, x, dsk_ref[:, xs], z_ref[:, xs], nw_ref[:, xs]).astype(ys_ref.dtype)


def _ssd_scan(ub, usm, conv_w, conv_b, dtb, alog, expand, dsk, nw, conv0, h0, *,
              n_seq, seq_rows, block_rows, carry, lead_null=0):
    m = ub.shape[0]
    ng = SSD_GROUPS_PER_STEP
    xw, sw = ng * GROUP_WIDTH, ng * D_STATE
    cw = xw + 2 * sw
    hpg = ng * HEADS_PER_GROUP
    if carry:
        nb = seq_rows // block_rows
        grid = (n_seq, SSD_GROUPS // ng, nb)
        c_mul = 1 if conv0.shape[0] == n_seq else 0
        h_mul = 1 if h0.shape[0] == n_seq else 0
        rmap = lambda off: (lambda b, g, c: (b * nb + c, off + g))
        smap = lambda b, g, c: (b * nb + c, 0)
        wmap = lambda off: (lambda b, g, c: (0, off + g))
        zmap = lambda b, g, c: (0, 0)
        exmap = lambda b, g, c: (g, 0, 0)
        cs_spec = lambda width, off: pl.BlockSpec((1, 8, width), lambda b, g, c: (b * c_mul, 0, off + g))
        h0_spec = pl.BlockSpec((1, hpg, SSD_HEAD_DIM, D_STATE), lambda b, g, c: (b * h_mul, g, 0, 0))
        ho_spec = pl.BlockSpec((1, hpg, SSD_HEAD_DIM, D_STATE), lambda b, g, c: (b, g, 0, 0))
        scratch = [pltpu.VMEM((block_rows + 8, cw), F32)]
        sem = ("parallel", "parallel", "arbitrary")
        kern = functools.partial(_ssd_carry_kernel, lead_null=lead_null)
    else:
        per = SEQS_PER_BLOCK
        grid = (m // block_rows, SSD_GROUPS // ng)
        rmap = lambda off: (lambda i, g: (i, off + g))
        smap = lambda i, g: (i, 0)
        wmap = lambda off: (lambda i, g: (0, off + g))
        zmap = lambda i, g: (0, 0)
        exmap = lambda i, g: (g, 0, 0)
        cs_spec = lambda width, off: pl.BlockSpec((per, CONV_W - 1, width), lambda i, g: (i, 0, off + g))
        h0_spec = ho_spec = pl.BlockSpec((per, hpg, SSD_HEAD_DIM, D_STATE), lambda i, g: (i, g, 0, 0))
        scratch = [pltpu.VMEM((8 * per + 8, cw), F32), pltpu.VMEM((8 * per, cw), F32),
                   pltpu.VMEM((block_rows, cw), F32)]
        sem = ("parallel", "parallel")
        kern = functools.partial(_ssd_multi_kernel, seg_len=seq_rows)
    cvb, cvc = SSD_WIDTH // sw, (SSD_WIDTH + BC_WIDTH) // sw
    in_specs = [
        pl.BlockSpec((block_rows, xw), rmap(UB_X // xw)), pl.BlockSpec((block_rows, sw), rmap(UB_B // sw)),
        pl.BlockSpec((block_rows, sw), rmap(UB_C // sw)), pl.BlockSpec((block_rows, xw), rmap(0)),
        pl.BlockSpec((block_rows, LANE), smap),
        pl.BlockSpec((CONV_W, xw), wmap(0)), pl.BlockSpec((CONV_W, sw), wmap(cvb)), pl.BlockSpec((CONV_W, sw), wmap(cvc)),
        pl.BlockSpec((1, xw), wmap(0)), pl.BlockSpec((1, sw), wmap(cvb)), pl.BlockSpec((1, sw), wmap(cvc)),
        pl.BlockSpec((1, LANE), zmap), pl.BlockSpec((1, LANE), zmap),
        pl.BlockSpec((ng, LANE, GROUP_WIDTH), exmap),
        pl.BlockSpec((1, xw), wmap(0)), pl.BlockSpec((1, xw), wmap(0)),
        cs_spec(xw, 0), cs_spec(sw, cvb), cs_spec(sw, cvc),
        h0_spec,
    ]
    out_specs = [pl.BlockSpec((block_rows, xw), rmap(0)), ho_spec]
    return pl.pallas_call(
        kern, grid=grid, in_specs=in_specs, out_specs=out_specs,
        out_shape=[jax.ShapeDtypeStruct((m, SSD_WIDTH), BF16),
                   jax.ShapeDtypeStruct((n_seq, SSD_HEADS, SSD_HEAD_DIM, D_STATE), F32)],
        scratch_shapes=scratch,
        compiler_params=pltpu.CompilerParams(dimension_semantics=sem, vmem_limit_bytes=VMEM_LIMIT),
        name="ssd_scan_carry" if carry else "ssd_scan_multi",
    )(ub, ub, ub, ub, usm, conv_w, conv_w, conv_w, conv_b, conv_b, conv_b, dtb, alog, expand, dsk, nw,
      conv0, conv0, conv0, h0)


def _matmul_kernel(x_ref, w_ref, o_ref, *wb_ref, act, w_t):
    w = w_ref[...].astype(BF16)
    if wb_ref:
        wb_ref[0][...] = w
    acc = _nt(x_ref[...], w) if w_t else _mm(x_ref[...], w)
    if act == "relu2":
        acc = jnp.square(jnp.maximum(acc, 0.0))
    o_ref[...] = acc.astype(o_ref.dtype)


def _matmul(x, w, *, tm, tn, out_dtype, act=None, name, n=None, emit=False, w_t=False, row0=0):
    m, k = x.shape
    n = w.shape[0 if w_t else 1] if n is None else n
    assert not emit or m == tm
    assert row0 == 0 or (w_t and row0 % 16 == 0)
    blk_spec = pl.BlockSpec((tn, k), lambda i, j: (j, 0)) if w_t else pl.BlockSpec((k, tn), lambda i, j: (0, j))
    w_spec = blk_spec if row0 == 0 else pl.BlockSpec((pl.Element(tn), pl.Element(k)),
                                                     lambda i, j: (pl.multiple_of(row0 + j * tn, 16), 0))
    out_specs = [pl.BlockSpec((tm, tn), lambda i, j: (i, j))]
    out_shape = [jax.ShapeDtypeStruct((m, n), out_dtype)]
    if emit:
        out_specs.append(blk_spec)
        out_shape.append(jax.ShapeDtypeStruct((n, k) if w_t else (k, n), BF16))
    res = pl.pallas_call(
        functools.partial(_matmul_kernel, act=act, w_t=w_t),
        grid=(m // tm, n // tn),
        in_specs=[pl.BlockSpec((tm, k), lambda i, j: (i, 0)), w_spec],
        out_specs=out_specs, out_shape=out_shape,
        compiler_params=pltpu.CompilerParams(dimension_semantics=("parallel", "parallel"),
                                             vmem_limit_bytes=VMEM_LIMIT),
        name=name,
    )(x, w)
    return res if emit else res[0]


def _small_proj_kernel(x_ref, w_ref, o_ref, xb_ref):
    xb = x_ref[...].astype(BF16)
    xb_ref[...] = xb
    o_ref[...] = _nt(xb, w_ref[...].astype(BF16))


def _small_proj(x, w, *, tm):
    m, k = x.shape
    return pl.pallas_call(
        _small_proj_kernel,
        grid=(m // tm,),
        in_specs=[pl.BlockSpec((tm, k), lambda i: (i, 0)), pl.BlockSpec((LANE, k), lambda i: (0, 0))],
        out_specs=[pl.BlockSpec((tm, LANE), lambda i: (i, 0)), pl.BlockSpec((tm, k), lambda i: (i, 0))],
        out_shape=[jax.ShapeDtypeStruct((m, LANE), F32), jax.ShapeDtypeStruct((m, k), BF16)],
        compiler_params=pltpu.CompilerParams(dimension_semantics=("parallel",), vmem_limit_bytes=VMEM_LIMIT),
        name="in_proj_small",
    )(x, w)


def _out_proj_kernel(a_ref, b_ref, wa_ref, wb_ref, res_ref, o_ref, *wo_refs):
    wa, wb = wa_ref[...], wb_ref[...]
    if wo_refs:
        wa, wb = wa.astype(BF16), wb.astype(BF16)
        wo_refs[0][...] = wa
        wo_refs[1][...] = wb
    o_ref[...] = ALPHA * res_ref[...] + _mm(a_ref[...], wa) + _mm(b_ref[...], wb)


def _out_proj(a, b, w_top, w_bot, res, *, tm, tn, emit=False):
    m, kh = a.shape
    n = w_top.shape[1]
    assert not emit or m == tm
    out_specs = [pl.BlockSpec((tm, tn), lambda i, j: (i, j))]
    out_shape = [jax.ShapeDtypeStruct((m, n), F32)]
    if emit:
        out_specs += [pl.BlockSpec((kh, tn), lambda i, j: (0, j))] * 2
        out_shape += [jax.ShapeDtypeStruct((kh, n), BF16)] * 2
    res_ = pl.pallas_call(
        _out_proj_kernel,
        grid=(m // tm, n // tn),
        in_specs=[pl.BlockSpec((tm, kh), lambda i, j: (i, 0)), pl.BlockSpec((tm, kh), lambda i, j: (i, 0)),
                  pl.BlockSpec((kh, tn), lambda i, j: (0, j)),
                  pl.BlockSpec((kh, tn), lambda i, j: (1 if emit else 0, j)),
                  pl.BlockSpec((tm, tn), lambda i, j: (i, j))],
        out_specs=out_specs, out_shape=out_shape,
        compiler_params=pltpu.CompilerParams(dimension_semantics=("parallel", "parallel"),
                                             vmem_limit_bytes=VMEM_LIMIT),
        name="out_proj",
    )(a, b, w_top, w_bot, res)
    return res_ if emit else res_[0]


def _down_proj_kernel(x_ref, w_ref, pre_ref, st_ref, g_ref, b_ref, o_ref, *wb_ref):
    w = w_ref[...].astype(BF16)
    if wb_ref:
        wb_ref[0][...] = w
    acc = _mm(x_ref[...], w)

    @pl.when(pl.program_id(2) == 0)
    def _():
        st = st_ref[...]
        h = (pre_ref[...] - st[:, 0:1]) * st[:, 1:2] * g_ref[...] + b_ref[...]
        o_ref[...] = ALPHA * h + acc

    @pl.when(pl.program_id(2) != 0)
    def _():
        o_ref[...] += acc


def _down_proj(x, w, pre, stats, g, b, *, tm, tn, tk, emit=False):
    m, k = x.shape
    n = w.shape[1]
    assert not emit or m == tm
    out_specs = [pl.BlockSpec((tm, tn), lambda i, j, kk: (i, j))]
    out_shape = [jax.ShapeDtypeStruct((m, n), F32)]
    if emit:
        out_specs.append(pl.BlockSpec((tk, tn), lambda i, j, kk: (kk, j)))
        out_shape.append(jax.ShapeDtypeStruct((k, n), BF16))
    res_ = pl.pallas_call(
        _down_proj_kernel,
        grid=(m // tm, n // tn, k // tk),
        in_specs=[pl.BlockSpec((tm, tk), lambda i, j, kk: (i, kk)), pl.BlockSpec((tk, tn), lambda i, j, kk: (kk, j)),
                  pl.BlockSpec((tm, tn), lambda i, j, kk: (i, j)), pl.BlockSpec((tm, LANE), lambda i, j, kk: (i, 0)),
                  pl.BlockSpec((1, tn), lambda i, j, kk: (0, j)), pl.BlockSpec((1, tn), lambda i, j, kk: (0, j))],
        out_specs=out_specs, out_shape=out_shape,
        compiler_params=pltpu.CompilerParams(dimension_semantics=("parallel", "parallel", "arbitrary"),
                                             vmem_limit_bytes=VMEM_LIMIT),
        name="down_proj",
    )(x, w, pre, stats, g, b)
    return res_ if emit else res_[0]


def _layer_norm_kernel(x_ref, g_ref, b_ref, o_ref, *st_ref):
    x = x_ref[...]
    mu = jnp.mean(x, axis=-1, keepdims=True)
    xc = x - mu
    var = jnp.mean(jnp.square(xc), axis=-1, keepdims=True)
    rstd = lax.rsqrt(var + EPS)
    o_ref[...] = (xc * rstd * g_ref[...] + b_ref[...]).astype(o_ref.dtype)
    if st_ref:
        lane = lax.broadcasted_iota(jnp.int32, st_ref[0].shape, 1)
        st_ref[0][...] = jnp.where(lane == 0, mu, jnp.where(lane == 1, rstd, 0.0))


def _layer_norm(x, g, b, *, tm, dtype, stats=False):
    m, d = x.shape
    out_specs = [pl.BlockSpec((tm, d), lambda i: (i, 0))]
    out_shape = [jax.ShapeDtypeStruct((m, d), dtype)]
    if stats:
        out_specs.append(pl.BlockSpec((tm, LANE), lambda i: (i, 0)))
        out_shape.append(jax.ShapeDtypeStruct((m, LANE), F32))
    res = pl.pallas_call(
        _layer_norm_kernel,
        grid=(m // tm,),
        in_specs=[pl.BlockSpec((tm, d), lambda i: (i, 0)), pl.BlockSpec((1, d), lambda i: (0, 0)),
                  pl.BlockSpec((1, d), lambda i: (0, 0))],
        out_specs=out_specs, out_shape=out_shape,
        compiler_params=pltpu.CompilerParams(dimension_semantics=("parallel",), vmem_limit_bytes=VMEM_LIMIT),
        name="layer_norm",
    )(x, g, b)
    return res if stats else res[0]


def _expand_matrix():
    e = np.zeros((SSD_GROUPS, LANE, GROUP_WIDTH), np.float32)
    for g in range(SSD_GROUPS):
        for j in range(HEADS_PER_GROUP):
            e[g, SM_DT + g * HEADS_PER_GROUP + j, j * SSD_HEAD_DIM:(j + 1) * SSD_HEAD_DIM] = 1.0
    return jnp.asarray(e, BF16)


def _lane_param(p):
    return jnp.zeros((1, LANE), F32).at[0, SM_DT:SM_DT + SSD_HEADS].set(p.astype(F32))


def _dense_tail(x, og, ys, prm, tm, emit):
    wb = dict(prm)
    tn = 512 if emit else 1024
    w_top, w_bot = (prm["w_out"], prm["w_out"]) if emit else prm["w_out"]
    pre1 = _out_proj(og, ys, w_top, w_bot, x, tm=tm, tn=tn, emit=emit)
    if emit:
        pre1, wt, wbt = pre1
        wb["w_out"] = (wt, wbt)
    hb, stats = _layer_norm(pre1, prm["ln1_g"], prm["ln1_b"], tm=min(tm, 256), dtype=BF16, stats=True)
    hid = _matmul(hb, prm["w_up"], tm=tm, tn=tn, out_dtype=BF16, act="relu2", name="mlp_up", emit=emit)
    if emit:
        hid, wb["w_up"] = hid
    pre2 = _down_proj(hid, prm["w_down"], pre1, stats, prm["ln1_g"], prm["ln1_b"], tm=tm, tn=1024,
                      tk=2048, emit=emit)
    if emit:
        pre2, wb["w_down"] = pre2
    out = _layer_norm(pre2, prm["ln2_g"], prm["ln2_b"], tm=min(tm, 256), dtype=F32)
    return out, wb


def kernel(x_prompt, x_sample, state_gla, state_ssm, state_conv, meta_tokens, w_in, w_gk_up, b_gk, gla_norm_w,
           conv_w, conv_b, dt_bias, a_log, d_skip, ssd_norm_w, w_out, ln1_g, ln1_b, w_up, w_down, ln2_g, ln2_b):
    bp, seq, d = x_prompt.shape
    bs, dec = x_sample.shape[:2]
    assert w_in.shape[0] == DEPTH and seq % CHUNK == 0 and dec == 4 and N_META <= CHUNK
    assert bs % SEQS_PER_BLOCK == 0
    wi = w_in[0]
    o_g = UA_WIDTH
    o_z = o_g + GATE_RANK
    o_dt = o_z + UB_WIDTH
    wt = wi.T
    w_small = jnp.concatenate([wt[o_g:o_z], wt[o_dt:], jnp.zeros((LANE - GATE_RANK - SSD_HEADS, d), F32)], axis=0)
    wg = jnp.concatenate([w_gk_up[0], jnp.zeros((LANE - GATE_RANK, GLA_QK), F32)], axis=0).astype(BF16)
    prm = dict(
        w_out=w_out[0], w_up=w_up[0], w_down=w_down[0],
        ln1_g=ln1_g[0][None], ln1_b=ln1_b[0][None], ln2_g=ln2_g[0][None], ln2_b=ln2_b[0][None],
    )
    gla_args = (wg, b_gk[0][None], gla_norm_w[0][None])
    ssd_args = (conv_w[0], conv_b[0][None], _lane_param(dt_bias[0]), _lane_param(a_log[0]), _expand_matrix(),
                jnp.repeat(d_skip[0], SSD_HEAD_DIM)[None, :], ssd_norm_w[0][None])
    tail = CONV_W - 1

    def in_proj(x, tm, w_ab):
        usm, xb = _small_proj(x, w_small, tm=min(tm, 512))
        if w_ab is None:
            ua, w_a = _matmul(xb, wt, tm=tm, tn=512, out_dtype=F32, name="in_proj_gla", n=UA_WIDTH, emit=True,
                              w_t=True)
            ub, w_b = _matmul(xb, wt, tm=tm, tn=512, out_dtype=F32, name="in_proj_ssd", n=UB_WIDTH, emit=True,
                              w_t=True, row0=o_z)
        else:
            w_a, w_b = w_ab
            ua = _matmul(xb, w_a, tm=tm, tn=1024, out_dtype=F32, name="in_proj_gla", w_t=True)
            ub = _matmul(xb, w_b, tm=tm, tn=1024, out_dtype=F32, name="in_proj_ssd", w_t=True)
        return ua, ub, usm, (w_a, w_b)

    xs = x_sample.reshape(bs * dec, d)
    ua_s, ub_s, usm_s, w_ab = in_proj(xs, bs * dec, None)
    blk = SEQS_PER_BLOCK * dec
    og_s, gla_s = _gla_scan(ua_s, usm_s, *gla_args, state_gla[0], n_seq=bs, seq_rows=dec, block_rows=blk,
                            carry=False)
    ys_s, ssm_s = _ssd_scan(ub_s, usm_s, *ssd_args, state_conv[0], state_ssm[0], n_seq=bs, seq_rows=dec,
                            block_rows=blk, carry=False)
    y_sample, prm_b = _dense_tail(xs, og_s, ys_s, prm, bs * dec, True)
    y_sample = y_sample.reshape(bs, dec, d)
    conv_s = ub_s[:, UB_X:].reshape(bs, dec, CONV_DIM)[:, dec - tail:]

    lead_null = CHUNK - N_META
    xm = jnp.concatenate([jnp.zeros((lead_null, d), F32), meta_tokens], axis=0)
    ua_m, ub_m, usm_m, _ = in_proj(xm, CHUNK, w_ab)
    _, gla_m = _gla_scan(ua_m, usm_m, *gla_args, jnp.zeros((1, GLA_HEADS, GLA_DK, GLA_DV), F32),
                         n_seq=1, seq_rows=CHUNK, block_rows=CHUNK, carry=True, lead_null=lead_null)
    _, ssm_m = _ssd_scan(ub_m, usm_m, *ssd_args, jnp.zeros((1, 8, CONV_DIM), F32),
                         jnp.zeros((1, SSD_HEADS, SSD_HEAD_DIM, D_STATE), F32),
                         n_seq=1, seq_rows=CHUNK, block_rows=CHUNK, carry=True, lead_null=lead_null)
    conv_m = ub_m[None, CHUNK - 8:, UB_X:]

    xp = x_prompt.reshape(bp * seq, d)
    ua_p, ub_p, usm_p, _ = in_proj(xp, 1024, w_ab)
    og_p, gla_p = _gla_scan(ua_p, usm_p, *gla_args, gla_m, n_seq=bp, seq_rows=seq, block_rows=256, carry=True)
    ys_p, ssm_p = _ssd_scan(ub_p, usm_p, *ssd_args, conv_m, ssm_m, n_seq=bp, seq_rows=seq, block_rows=256,
                            carry=True)
    y_prompt, _ = _dense_tail(xp, og_p, ys_p, prm_b, 1024, False)
    y_prompt = y_prompt.reshape(bp, seq, d)
    conv_p = jnp.stack([ub_p[(b + 1) * seq - tail:(b + 1) * seq, UB_X:] for b in range(bp)])

    return (y_prompt, y_sample, gla_p[None], ssm_p[None], conv_p[None], gla_s[None], ssm_s[None], conv_s[None])
```

```python
import functools

import numpy as np
import jax
import jax.numpy as jnp
from jax import lax
from jax.experimental import pallas as pl
from jax.experimental.pallas import tpu as pltpu

F32 = jnp.float32
BF16 = jnp.bfloat16

GLA_HEADS = 4
GLA_DK = 256
GLA_DV = 512
GLA_QK = GLA_HEADS * GLA_DK
GLA_WIDTH = GLA_HEADS * GLA_DV
GATE_RANK = 16
GATE_NORM = 16.0
SSD_GROUPS = 8
HEADS_PER_GROUP = 4
SSD_HEAD_DIM = 64
SSD_HEADS = SSD_GROUPS * HEADS_PER_GROUP
GROUP_WIDTH = HEADS_PER_GROUP * SSD_HEAD_DIM
SSD_WIDTH = SSD_GROUPS * GROUP_WIDTH
D_STATE = 128
BC_WIDTH = SSD_GROUPS * D_STATE
CONV_W = 4
CONV_DIM = SSD_WIDTH + 2 * BC_WIDTH
N_META = 16
CHUNK = 64
CHUNK_SHIFT = 6
EPS = 1e-5
DEPTH = 1
ALPHA = (2.0 * DEPTH) ** 0.25

UA_K = GLA_QK
UA_V = 2 * GLA_QK
UA_R = UA_V + GLA_WIDTH
UA_WIDTH = UA_R + GLA_WIDTH
UB_X = SSD_WIDTH
UB_B = UB_X + SSD_WIDTH
UB_C = UB_B + BC_WIDTH
UB_WIDTH = UB_C + BC_WIDTH
LANE = 128
SM_DT = GATE_RANK

GLA_HEADS_PER_STEP = 4
SSD_GROUPS_PER_STEP = 4
SEQS_PER_BLOCK = 4

VMEM_LIMIT = 56 * 1024 * 1024


def _mm(a, b):
    return jnp.dot(a, b, preferred_element_type=F32)


def _nt(a, b):
    return lax.dot_general(a, b, (((1,), (1,)), ((), ())), preferred_element_type=F32)


def _split2(x):
    hi = x.astype(BF16)
    lo = (x - hi.astype(F32)).astype(BF16)
    return hi, lo


def _mm_split(m01, x):
    hi, lo = _split2(x)
    return _mm(m01, hi) + _mm(m01, lo)


def _pad_rows(x, n):
    t = x.shape[0]
    if t >= n:
        return x
    return jnp.concatenate([x, jnp.zeros((n - t, x.shape[1]), x.dtype)], axis=0)


def _silu(x):
    return x * (1.0 / (1.0 + jnp.exp(-x)))


def _softplus(x):
    return jnp.maximum(x, 0.0) + jnp.log(1.0 + jnp.exp(-jnp.abs(x)))


def _f01(m):
    return jnp.where(m, 1.0, 0.0).astype(BF16)


def _eye_bf16(n):
    ri = lax.broadcasted_iota(jnp.int32, (n, n), 0)
    ci = lax.broadcasted_iota(jnp.int32, (n, n), 1)
    return _f01(ri == ci)


def _split3_rows(x, period):
    hi = x.astype(BF16)
    r1 = x - hi.astype(F32)
    mid = r1.astype(BF16)
    lo = (r1 - mid.astype(F32)).astype(BF16)
    ridx = lax.broadcasted_iota(jnp.int32, x.shape, 0) & (period - 1)
    out = jnp.where(ridx == 0, hi.astype(F32),
                    jnp.where(ridx == 1, mid.astype(F32),
                              jnp.where(ridx == 2, lo.astype(F32), 0.0)))
    return out.astype(BF16)


def _lane_group_sum(t, shift, idx):
    lane = lax.broadcasted_iota(jnp.int32, t.shape, 1)
    return jnp.sum(jnp.where((lane >> shift) == idx, t, 0.0), axis=1, keepdims=True)


def _chunk_rows_only(xb, c):
    rows_p, w = xb.shape
    lo, hi = c * CHUNK, (c + 1) * CHUNK
    pieces = []
    if lo:
        pieces.append(jnp.zeros((lo, w), xb.dtype))
    pieces.append(xb[lo:hi])
    if hi < rows_p:
        pieces.append(jnp.zeros((rows_p - hi, w), xb.dtype))
    return pieces[0] if len(pieces) == 1 else jnp.concatenate(pieces, axis=0)


def _lane_group_keep(t, shift, idx):
    lane = lax.broadcasted_iota(jnp.int32, t.shape, 1)
    return jnp.where((lane >> shift) == idx, t, 0.0).astype(BF16)


def _chunk_masks(rows, cols, lead_null, lane_mod=None):
    ri = lax.broadcasted_iota(jnp.int32, (rows, cols), 0)
    ci = lax.broadcasted_iota(jnp.int32, (rows, cols), 1)
    if lane_mod is None:
        same = (ri >> CHUNK_SHIFT) == (ci >> CHUNK_SHIFT)
        cl = ci & (CHUNK - 1)
    else:
        cl = ci & lane_mod
        same = None
    rl = ri & (CHUNK - 1)
    valid = cl >= lead_null if lead_null else None
    base = same
    if valid is not None:
        base = valid if base is None else (base & valid)
    causal = (cl <= rl) if base is None else (base & (cl <= rl))
    middle = (lead_null + CHUNK) // 2 - 1
    upto_mid = (cl <= middle) if base is None else (base & (cl <= middle))
    return base, causal, upto_mid


def _seg_masks(tb, cols, seg_len, lane_mod=None):
    ri = lax.broadcasted_iota(jnp.int32, (tb, cols), 0)
    ci = lax.broadcasted_iota(jnp.int32, (tb, cols), 1)
    if lane_mod is not None:
        ci = ci & lane_mod
    shift = seg_len.bit_length() - 1
    same = ((ri >> shift) == (ci >> shift)) & (ci < tb)
    causal = same & (ci <= ri)
    upto_mid = same & ((ci & (seg_len - 1)) <= seg_len // 2 - 1)
    return same, causal, upto_mid


def _gla_gate(sm_b, wg, bg):
    gpre = _mm(sm_b, wg) + bg
    return -_softplus(-gpre) * (1.0 / GATE_NORM)


def _gla_finish(o, gnw, r):
    o = o * lax.rsqrt(jnp.mean(jnp.square(o), axis=-1, keepdims=True) + EPS)
    return o * gnw * _silu(r)


def _gla_block(q, k, v, g, mats, causal_p, eye):
    tri, midm, allm = mats
    rows, dk = q.shape
    rows_p = max(rows, LANE)
    bc = _mm_split(tri, g)
    anchor = _mm_split(midm, g)
    blast = _mm_split(allm, g)
    qs = q * (dk ** -0.5)
    qt = (qs * jnp.exp(bc - anchor)).astype(BF16)
    kt = _pad_rows((k * jnp.exp(anchor - bc)).astype(BF16), rows_p)
    p = jnp.where(causal_p, _nt(qt, kt), 0.0)
    vb = _pad_rows(v.astype(BF16), rows_p)
    o = _mm(p.astype(BF16), vb)
    qb = (qs * jnp.exp(bc)).astype(BF16)
    kd = _pad_rows((k * jnp.exp(blast - bc)).astype(BF16), rows_p)
    kd_t = _nt(eye, kd)
    return o, qb, kd_t, vb, jnp.exp(blast)


def _gla_carry_kernel(q_ref, k_ref, v_ref, r_ref, sm_ref, wg_ref, bg_ref, gnw_ref, s0_ref,
                      og_ref, s_ref, *, lead_null):
    rows = q_ref.shape[0]
    nchunk = rows // CHUNK
    rows_p = max(rows, LANE)

    @pl.when(pl.program_id(2) == 0)
    def _():
        s_ref[...] = s0_ref[...]

    eye = _eye_bf16(GLA_DK)
    same, causal, upto_mid = _chunk_masks(rows, rows, lead_null)
    mats = (_f01(causal), _f01(upto_mid), _f01(same))
    causal_p = _chunk_masks(rows, rows_p, lead_null)[1]
    sm_b = sm_ref[...].astype(BF16)
    for hh in range(GLA_HEADS_PER_STEP):
        ks = slice(hh * GLA_DK, (hh + 1) * GLA_DK)
        vs = slice(hh * GLA_DV, (hh + 1) * GLA_DV)
        g = _gla_gate(sm_b, wg_ref[:, ks], bg_ref[:, ks])
        k = k_ref[:, ks]
        if lead_null:
            rl = lax.broadcasted_iota(jnp.int32, (rows, 1), 0) & (CHUNK - 1)
            k = jnp.where(rl >= lead_null, k, 0.0)
        o, qb, kd_t, vb, e_last = _gla_block(q_ref[:, ks], k, v_ref[:, vs], g, mats, causal_p, eye)
        e_rows = [_split3_rows(e_last[c * CHUNK:c * CHUNK + 16], 16) for c in range(nchunk)]
        e_rows.append(jnp.zeros((LANE - 16 * nchunk, GLA_DK), BF16))
        e_t = _nt(eye, jnp.concatenate(e_rows, axis=0))
        s = s_ref[0, hh]
        kd_tb = kd_t.astype(BF16)
        inter = []
        for c in range(nchunk):
            inter.append(_mm(qb[c * CHUNK:(c + 1) * CHUNK], s.astype(BF16)))
            s = _lane_group_sum(e_t, 4, c) * s + _mm(kd_tb, _chunk_rows_only(vb, c))
        s_ref[0, hh] = s
        o = o + (inter[0] if nchunk == 1 else jnp.concatenate(inter, axis=0))
        og_ref[:, vs] = _gla_finish(o, gnw_ref[...], r_ref[:, vs]).astype(og_ref.dtype)


def _gla_multi_kernel(q_ref, k_ref, v_ref, r_ref, sm_ref, wg_ref, bg_ref, gnw_ref, s0_ref,
                      og_ref, s_ref, *, seg_len, nh=GLA_HEADS_PER_STEP):
    tb = q_ref.shape[0]
    nseq = tb // seg_len
    seg_shift = seg_len.bit_length() - 1
    eye = _eye_bf16(GLA_DK)
    same, causal, upto_mid = _seg_masks(tb, tb, seg_len)
    mats = (_f01(causal), _f01(upto_mid), _f01(same))
    causal_p = _seg_masks(tb, LANE, seg_len)[1]
    sm_b = sm_ref[...].astype(BF16)
    for hh in range(nh):
        ks = slice(hh * GLA_DK, (hh + 1) * GLA_DK)
        vs = slice(hh * GLA_DV, (hh + 1) * GLA_DV)
        g = _gla_gate(sm_b, wg_ref[:, ks], bg_ref[:, ks])
        o, qb, kd_t, vb, e_last = _gla_block(q_ref[:, ks], k_ref[:, ks], v_ref[:, vs], g, mats, causal_p, eye)
        e_rows = jnp.concatenate([_split3_rows(e_last, seg_len), jnp.zeros((LANE - tb, GLA_DK), BF16)], axis=0)
        e_t = _nt(eye, e_rows)
        rowseg = lax.broadcasted_iota(jnp.int32, (tb, 1), 0) >> seg_shift
        for i in range(nseq):
            s = s0_ref[i, hh]
            o = o + jnp.where(rowseg == i, _mm(qb, s.astype(BF16)), 0.0)
            s_ref[i, hh] = (_lane_group_sum(e_t, seg_shift, i) * s
                            + _mm(_lane_group_keep(kd_t, seg_shift, i), vb))
        og_ref[:, vs] = _gla_finish(o, gnw_ref[...], r_ref[:, vs]).astype(og_ref.dtype)


def _gla_scan(ua, usm, wg, bg, gnw, s0, *, n_seq, seq_rows, block_rows, carry, lead_null=0, row0=0):
    m = n_seq * seq_rows
    r0 = row0 // block_rows
    nh = GLA_HEADS_PER_STEP
    kw, vw = nh * GLA_DK, nh * GLA_DV
    if carry:
        nb = seq_rows // block_rows
        grid = (n_seq, GLA_HEADS // nh, nb)
        s0_mul = 1 if s0.shape[0] == n_seq else 0
        rmap = lambda off: (lambda b, h, c: (r0 + b * nb + c, off + h))
        smap = lambda b, h, c: (r0 + b * nb + c, 0)
        omap = lambda b, h, c: (b * nb + c, h)
        cmap = lambda b, h, c: (0, h)
        zmap = lambda b, h, c: (0, 0)
        s0_spec = pl.BlockSpec((1, nh, GLA_DK, GLA_DV), lambda b, h, c: (b * s0_mul, h, 0, 0))
        so_spec = pl.BlockSpec((1, nh, GLA_DK, GLA_DV), lambda b, h, c: (b, h, 0, 0))
        sem = ("parallel", "parallel", "arbitrary")
        kern = functools.partial(_gla_carry_kernel, lead_null=lead_null)
    else:
        grid = (m // block_rows, GLA_HEADS // nh)
        rmap = lambda off: (lambda i, h: (r0 + i, off + h))
        smap = lambda i, h: (r0 + i, 0)
        omap = lambda i, h: (i, h)
        cmap = lambda i, h: (0, h)
        zmap = lambda i, h: (0, 0)
        s0_spec = so_spec = pl.BlockSpec((SEQS_PER_BLOCK, nh, GLA_DK, GLA_DV), lambda i, h: (i, h, 0, 0))
        sem = ("parallel", "parallel")
        kern = functools.partial(_gla_multi_kernel, seg_len=seq_rows)
    in_specs = [
        pl.BlockSpec((block_rows, kw), rmap(0)), pl.BlockSpec((block_rows, kw), rmap(UA_K // kw)),
        pl.BlockSpec((block_rows, vw), rmap(UA_V // vw)), pl.BlockSpec((block_rows, vw), rmap(UA_R // vw)),
        pl.BlockSpec((block_rows, LANE), smap),
        pl.BlockSpec((LANE, kw), cmap), pl.BlockSpec((1, kw), cmap), pl.BlockSpec((1, GLA_DV), zmap),
        s0_spec,
    ]
    out_specs = [pl.BlockSpec((block_rows, vw), omap), so_spec]
    return pl.pallas_call(
        kern, grid=grid, in_specs=in_specs, out_specs=out_specs,
        out_shape=[jax.ShapeDtypeStruct((m, GLA_WIDTH), BF16),
                   jax.ShapeDtypeStruct((n_seq, GLA_HEADS, GLA_DK, GLA_DV), F32)],
        compiler_params=pltpu.CompilerParams(dimension_semantics=sem, vmem_limit_bytes=VMEM_LIMIT),
        name="gla_scan_carry" if carry else "gla_scan_multi",
    )(ua, ua, ua, ua, usm, wg, bg, gnw, s0)


def _ssd_block(x, dt, dta, expand, tri, allm, causal_l, eye, row_valid):
    rows, gw = x.shape
    rows_p = max(rows, LANE)
    p_hi, p_lo = [_mm(p, expand).astype(BF16) for p in _split2(dta)]
    cb = _mm(tri, p_hi) + _mm(tri, p_lo)
    cl = _mm(allm, p_hi) + _mm(allm, p_lo)
    ri = lax.broadcasted_iota(jnp.int32, (rows, gw), 0)
    li = lax.broadcasted_iota(jnp.int32, (rows, gw), 1)
    diag = jnp.where((li & (CHUNK - 1)) == (ri & (CHUNK - 1)), cb, 0.0)
    rw = _mm_split(allm, diag)
    decay = jnp.exp(jnp.where(causal_l, cb - rw, -jnp.inf))
    dt_hi, dt_lo = _split2(dt)
    xdt = x * (_mm(dt_hi, expand) + _mm(dt_lo, expand))
    if row_valid is not None:
        xdt = jnp.where(row_valid, xdt, 0.0)
    xw_t = _nt(eye, _pad_rows((xdt * jnp.exp(cl - cb)).astype(BF16), rows_p))
    return decay, xdt, jnp.exp(cb), xw_t, jnp.exp(cl)


def _ssd_intra(cm_b, bm_b, decay, xdt, seg_rows):
    gw = xdt.shape[1]
    b64 = _pad_rows(bm_b, CHUNK)
    cb4 = _nt(cm_b, jnp.concatenate([b64] * HEADS_PER_GROUP, axis=0))
    x64 = _pad_rows(xdt, CHUNK)
    r2 = lax.broadcasted_iota(jnp.int32, (gw, gw), 0)
    l2 = lax.broadcasted_iota(jnp.int32, (gw, gw), 1)
    bd = jnp.where((r2 >> CHUNK_SHIFT) == (l2 >> CHUNK_SHIFT),
                   jnp.concatenate([x64] * HEADS_PER_GROUP, axis=0), 0.0)
    return _mm((cb4 * decay).astype(BF16), bd.astype(BF16))


def _conv_rows(xpad, cw_refs, cb_refs, n):
    w = [jnp.concatenate([r[pl.ds(i, 1), :] for r in cw_refs], axis=1) for i in range(CONV_W)]
    bias = jnp.concatenate([r[...] for r in cb_refs], axis=1)
    xa = xpad[pl.ds(0, n + 8), :]
    x1 = pltpu.roll(xa, 1, axis=0)
    a = w[3] * xa + w[2] * x1
    b = w[1] * xa + w[0] * x1
    y = a + pltpu.roll(b, 2, axis=0) + bias
    return y[8:]


def _ssd_finish(y, x, dsk, z, nw):
    y = y + dsk * x
    y = y * _silu(z)
    y = y * lax.rsqrt(jnp.mean(jnp.square(y), axis=-1, keepdims=True) + EPS)
    return y * nw


def _ssd_lanes():
    ng = SSD_GROUPS_PER_STEP
    xw, sw = ng * GROUP_WIDTH, ng * D_STATE
    return ((0, xw), (xw, xw + sw), (xw + sw, xw + 2 * sw))


def _ssd_carry_kernel(x_ref, b_ref, c_ref, z_ref, sm_ref, cwx_ref, cwb_ref, cwc_ref, cbx_ref, cbb_ref, cbc_ref,
                      dtb_ref, alog_ref, ex_ref, dsk_ref, nw_ref, csx_ref, csb_ref, csc_ref, h0_ref,
                      ys_ref, h_ref, xpad, *, lead_null):
    rows = x_ref.shape[0]
    nchunk = rows // CHUNK
    rows_p = max(rows, LANE)
    gw = GROUP_WIDTH
    lanes = _ssd_lanes()

    @pl.when(pl.program_id(2) == 0)
    def _():
        h_ref[...] = h0_ref[...]
        for (lo, hi), cs in zip(lanes, (csx_ref, csb_ref, csc_ref)):
            xpad[pl.ds(0, 8), lo:hi] = cs[0]

    for (lo, hi), src in zip(lanes, (x_ref, b_ref, c_ref)):
        xpad[pl.ds(8, rows), lo:hi] = src[...]
    xc = _silu(_conv_rows(xpad, (cwx_ref, cwb_ref, cwc_ref), (cbx_ref, cbb_ref, cbc_ref), rows))
    xpad[pl.ds(0, 8), :] = xpad[pl.ds(rows, 8), :]

    eye = _eye_bf16(gw)
    same, causal, _ = _chunk_masks(rows, rows, lead_null)
    tri, allm = _f01(causal), _f01(same)
    causal_l = _chunk_masks(rows, gw, lead_null, lane_mod=CHUNK - 1)[1]
    row_valid = None
    if lead_null:
        row_valid = (lax.broadcasted_iota(jnp.int32, (rows, 1), 0) & (CHUNK - 1)) >= lead_null
    dt = _softplus(sm_ref[...] + dtb_ref[...])
    dta = dt * (-jnp.exp(alog_ref[...]))
    for gg in range(SSD_GROUPS_PER_STEP):
        xs = slice(gg * gw, (gg + 1) * gw)
        x = xc[:, lanes[0][0] + gg * gw:lanes[0][0] + (gg + 1) * gw]
        bm_b = xc[:, lanes[1][0] + gg * D_STATE:lanes[1][0] + (gg + 1) * D_STATE].astype(BF16)
        cm_b = xc[:, lanes[2][0] + gg * D_STATE:lanes[2][0] + (gg + 1) * D_STATE].astype(BF16)
        decay, xdt, ecb, xw_t, e_last = _ssd_block(x, dt, dta, ex_ref[gg], tri, allm, causal_l, eye, row_valid)
        bpad = _pad_rows(bm_b, rows_p)
        e_rows = [_split3_rows(e_last[c * CHUNK:c * CHUNK + 16], 16) for c in range(nchunk)]
        e_rows.append(jnp.zeros((LANE - 16 * nchunk, gw), BF16))
        e_t = _nt(eye, jnp.concatenate(e_rows, axis=0))
        hs = slice(gg * HEADS_PER_GROUP, (gg + 1) * HEADS_PER_GROUP)
        h = h_ref[0, hs].reshape(gw, D_STATE)
        xw_tb = xw_t.astype(BF16)
        ys = []
        for c in range(nchunk):
            rs = slice(c * CHUNK, (c + 1) * CHUNK)
            y = _ssd_intra(cm_b[rs], bm_b[rs], decay[rs], xdt[rs], CHUNK)
            ys.append(y + _nt(cm_b[rs], h.astype(BF16)) * ecb[rs])
            h = _lane_group_sum(e_t, 4, c) * h + _mm(xw_tb, _chunk_rows_only(bpad, c))
        h_ref[0, hs] = h.reshape(HEADS_PER_GROUP, SSD_HEAD_DIM, D_STATE)
        y = ys[0] if nchunk == 1 else jnp.concatenate(ys, axis=0)
        ys_ref[:, xs] = _ssd_finish(y, x, dsk_ref[:, xs], z_ref[:, xs], nw_ref[:, xs]).astype(ys_ref.dtype)


def _ssd_multi_kernel(x_ref, b_ref, c_ref, z_ref, sm_ref, cwx_ref, cwb_ref, cwc_ref, cbx_ref, cbb_ref, cbc_ref,
                      dtb_ref, alog_ref, ex_ref, dsk_ref, nw_ref, csx_ref, csb_ref, csc_ref, h0_ref,
                      ys_ref, h_ref, xpad, cres, xcs, *, seg_len):
    tb = x_ref.shape[0]
    nseq = tb // seg_len
    seg_shift = seg_len.bit_length() - 1
    gw = GROUP_WIDTH
    lanes = _ssd_lanes()
    xpad[...] = jnp.zeros(xpad.shape, F32)
    for i in range(nseq):
        for (lo, hi), src, cs in zip(lanes, (x_ref, b_ref, c_ref), (csx_ref, csb_ref, csc_ref)):
            xpad[pl.ds(8 + 8 * i + 1, CONV_W - 1), lo:hi] = cs[i]
            xpad[pl.ds(8 + 8 * i + 4, seg_len), lo:hi] = src[pl.ds(seg_len * i, seg_len), :]
    cres[...] = _conv_rows(xpad, (cwx_ref, cwb_ref, cwc_ref), (cbx_ref, cbb_ref, cbc_ref), 8 * nseq)
    for i in range(nseq):
        xcs[pl.ds(seg_len * i, seg_len), :] = cres[pl.ds(8 * i + 4, seg_len), :]
    xc = _silu(xcs[...])

    eye = _eye_bf16(gw)
    same, causal, _ = _seg_masks(tb, tb, seg_len)
    tri, allm = _f01(causal), _f01(same)
    causal_l = _seg_masks(tb, gw, seg_len, lane_mod=CHUNK - 1)[1]
    dt = _softplus(sm_ref[...] + dtb_ref[...])
    dta = dt * (-jnp.exp(alog_ref[...]))
    rowseg = lax.broadcasted_iota(jnp.int32, (tb, 1), 0) >> seg_shift
    for gg in range(SSD_GROUPS_PER_STEP):
        xs = slice(gg * gw, (gg + 1) * gw)
        x = xc[:, lanes[0][0] + gg * gw:lanes[0][0] + (gg + 1) * gw]
        bm_b = xc[:, lanes[1][0] + gg * D_STATE:lanes[1][0] + (gg + 1) * D_STATE].astype(BF16)
        cm_b = xc[:, lanes[2][0] + gg * D_STATE:lanes[2][0] + (gg + 1) * D_STATE].astype(BF16)
        decay, xdt, ecb, xw_t, e_last = _ssd_block(x, dt, dta, ex_ref[gg], tri, allm, causal_l, eye, None)
        bpad = _pad_rows(bm_b, LANE)
        e_rows = jnp.concatenate([_split3_rows(e_last, seg_len), jnp.zeros((LANE - tb, gw), BF16)], axis=0)
        e_t = _nt(eye, e_rows)
        y = _ssd_intra(cm_b, bm_b, decay, xdt, tb)
        hs = slice(gg * HEADS_PER_GROUP, (gg + 1) * HEADS_PER_GROUP)
        for i in range(nseq):
            h = h0_ref[i, hs].reshape(gw, D_STATE)
            y = y + jnp.where(rowseg == i, _nt(cm_b, h.astype(BF16)) * ecb, 0.0)
            h = _lane_group_sum(e_t, seg_shift, i) * h + _mm(_lane_group_keep(xw_t, seg_shift, i), bpad)
            h_ref[i, hs] = h.reshape(HEADS_PER_GROUP, SSD_HEAD_DIM, D_STATE)
        ys_ref[:, xs] = _ssd_finish(y, x, dsk_ref[:, xs], z_ref[:, xs], nw_ref[:, xs]).astype(ys_ref.dtype)


def _ssd_scan(ub, usm, conv_w, conv_b, dtb, alog, expand, dsk, nw, conv0, h0, *,
              n_seq, seq_rows, block_rows, carry, lead_null=0, row0=0):
    m = n_seq * seq_rows
    r0 = row0 // block_rows
    ng = SSD_GROUPS_PER_STEP
    xw, sw = ng * GROUP_WIDTH, ng * D_STATE
    cw = xw + 2 * sw
    hpg = ng * HEADS_PER_GROUP
    if carry:
        nb = seq_rows // block_rows
        grid = (n_seq, SSD_GROUPS // ng, nb)
        c_mul = 1 if conv0.shape[0] == n_seq else 0
        h_mul = 1 if h0.shape[0] == n_seq else 0
        rmap = lambda off: (lambda b, g, c: (r0 + b * nb + c, off + g))
        smap = lambda b, g, c: (r0 + b * nb + c, 0)
        omap = lambda b, g, c: (b * nb + c, g)
        wmap = lambda off: (lambda b, g, c: (0, off + g))
        zmap = lambda b, g, c: (0, 0)
        exmap = lambda b, g, c: (g, 0, 0)
        cs_spec = lambda width, off: pl.BlockSpec((1, 8, width), lambda b, g, c: (b * c_mul, 0, off + g))
        h0_spec = pl.BlockSpec((1, hpg, SSD_HEAD_DIM, D_STATE), lambda b, g, c: (b * h_mul, g, 0, 0))
        ho_spec = pl.BlockSpec((1, hpg, SSD_HEAD_DIM, D_STATE), lambda b, g, c: (b, g, 0, 0))
        scratch = [pltpu.VMEM((block_rows + 8, cw), F32)]
        sem = ("parallel", "parallel", "arbitrary")
        kern = functools.partial(_ssd_carry_kernel, lead_null=lead_null)
    else:
        per = SEQS_PER_BLOCK
        grid = (m // block_rows, SSD_GROUPS // ng)
        rmap = lambda off: (lambda i, g: (r0 + i, off + g))
        smap = lambda i, g: (r0 + i, 0)
        omap = lambda i, g: (i, g)
        wmap = lambda off: (lambda i, g: (0, off + g))
        zmap = lambda i, g: (0, 0)
        exmap = lambda i, g: (g, 0, 0)
        cs_spec = lambda width, off: pl.BlockSpec((per, CONV_W - 1, width), lambda i, g: (i, 0, off + g))
        h0_spec = ho_spec = pl.BlockSpec((per, hpg, SSD_HEAD_DIM, D_STATE), lambda i, g: (i, g, 0, 0))
        scratch = [pltpu.VMEM((8 * per + 8, cw), F32), pltpu.VMEM((8 * per, cw), F32),
                   pltpu.VMEM((block_rows, cw), F32)]
        sem = ("parallel", "parallel")
        kern = functools.partial(_ssd_multi_kernel, seg_len=seq_rows)
    cvb, cvc = SSD_WIDTH // sw, (SSD_WIDTH + BC_WIDTH) // sw
    in_specs = [
        pl.BlockSpec((block_rows, xw), rmap(UB_X // xw)), pl.BlockSpec((block_rows, sw), rmap(UB_B // sw)),
        pl.BlockSpec((block_rows, sw), rmap(UB_C // sw)), pl.BlockSpec((block_rows, xw), rmap(0)),
        pl.BlockSpec((block_rows, LANE), smap),
        pl.BlockSpec((CONV_W, xw), wmap(0)), pl.BlockSpec((CONV_W, sw), wmap(cvb)), pl.BlockSpec((CONV_W, sw), wmap(cvc)),
        pl.BlockSpec((1, xw), wmap(0)), pl.BlockSpec((1, sw), wmap(cvb)), pl.BlockSpec((1, sw), wmap(cvc)),
        pl.BlockSpec((1, LANE), zmap), pl.BlockSpec((1, LANE), zmap),
        pl.BlockSpec((ng, LANE, GROUP_WIDTH), exmap),
        pl.BlockSpec((1, xw), wmap(0)), pl.BlockSpec((1, xw), wmap(0)),
        cs_spec(xw, 0), cs_spec(sw, cvb), cs_spec(sw, cvc),
        h0_spec,
    ]
    out_specs = [pl.BlockSpec((block_rows, xw), omap), ho_spec]
    return pl.pallas_call(
        kern, grid=grid, in_specs=in_specs, out_specs=out_specs,
        out_shape=[jax.ShapeDtypeStruct((m, SSD_WIDTH), BF16),
                   jax.ShapeDtypeStruct((n_seq, SSD_HEADS, SSD_HEAD_DIM, D_STATE), F32)],
        scratch_shapes=scratch,
        compiler_params=pltpu.CompilerParams(dimension_semantics=sem, vmem_limit_bytes=VMEM_LIMIT),
        name="ssd_scan_carry" if carry else "ssd_scan_multi",
    )(ub, ub, ub, ub, usm, conv_w, conv_w, conv_w, conv_b, conv_b, conv_b, dtb, alog, expand, dsk, nw,
      conv0, conv0, conv0, h0)


N_GLA_IN = 9
N_SSD_IN = 20
MULTI_GLA_HEADS = 2


def _mixers_multi_kernel(*refs, seg_len):
    gla_in = refs[:N_GLA_IN]
    ssd_in = refs[N_GLA_IN:N_GLA_IN + N_SSD_IN]
    og_ref, s_ref, ys_ref, h_ref, xpad, cres, xcs = refs[N_GLA_IN + N_SSD_IN:]
    _gla_multi_kernel(*gla_in, og_ref, s_ref, seg_len=seg_len, nh=MULTI_GLA_HEADS)
    _ssd_multi_kernel(*ssd_in, ys_ref, h_ref, xpad, cres, xcs, seg_len=seg_len)


def _mixers_multi(ua, ub, usm, gla_args, ssd_args, s0, conv0, h0, *, n_seq, seq_rows):
    per = SEQS_PER_BLOCK
    block_rows = per * seq_rows
    m = n_seq * seq_rows
    nh, ng = MULTI_GLA_HEADS, SSD_GROUPS_PER_STEP
    assert GLA_HEADS // nh == SSD_GROUPS // ng
    kw, vw = nh * GLA_DK, nh * GLA_DV
    xw, sw = ng * GROUP_WIDTH, ng * D_STATE
    cw = xw + 2 * sw
    hpg = ng * HEADS_PER_GROUP
    cvb, cvc = SSD_WIDTH // sw, (SSD_WIDTH + BC_WIDTH) // sw
    rmap = lambda off: (lambda i, p: (i, off + p))
    smap = lambda i, p: (i, 0)
    wmap = lambda off: (lambda i, p: (0, off + p))
    zmap = lambda i, p: (0, 0)
    cs_spec = lambda width, off: pl.BlockSpec((per, CONV_W - 1, width), lambda i, p: (i, 0, off + p))
    s_spec = pl.BlockSpec((per, nh, GLA_DK, GLA_DV), lambda i, p: (i, p, 0, 0))
    h_spec = pl.BlockSpec((per, hpg, SSD_HEAD_DIM, D_STATE), lambda i, p: (i, p, 0, 0))
    gla_specs = [
        pl.BlockSpec((block_rows, kw), rmap(0)), pl.BlockSpec((block_rows, kw), rmap(UA_K // kw)),
        pl.BlockSpec((block_rows, vw), rmap(UA_V // vw)), pl.BlockSpec((block_rows, vw), rmap(UA_R // vw)),
        pl.BlockSpec((block_rows, LANE), smap),
        pl.BlockSpec((LANE, kw), wmap(0)), pl.BlockSpec((1, kw), wmap(0)), pl.BlockSpec((1, GLA_DV), zmap),
        s_spec,
    ]
    ssd_specs = [
        pl.BlockSpec((block_rows, xw), rmap(UB_X // xw)), pl.BlockSpec((block_rows, sw), rmap(UB_B // sw)),
        pl.BlockSpec((block_rows, sw), rmap(UB_C // sw)), pl.BlockSpec((block_rows, xw), rmap(0)),
        pl.BlockSpec((block_rows, LANE), smap),
        pl.BlockSpec((CONV_W, xw), wmap(0)), pl.BlockSpec((CONV_W, sw), wmap(cvb)), pl.BlockSpec((CONV_W, sw), wmap(cvc)),
        pl.BlockSpec((1, xw), wmap(0)), pl.BlockSpec((1, sw), wmap(cvb)), pl.BlockSpec((1, sw), wmap(cvc)),
        pl.BlockSpec((1, LANE), zmap), pl.BlockSpec((1, LANE), zmap),
        pl.BlockSpec((ng, LANE, GROUP_WIDTH), lambda i, p: (p, 0, 0)),
        pl.BlockSpec((1, xw), wmap(0)), pl.BlockSpec((1, xw), wmap(0)),
        cs_spec(xw, 0), cs_spec(sw, cvb), cs_spec(sw, cvc),
        h_spec,
    ]
    assert len(gla_specs) == N_GLA_IN and len(ssd_specs) == N_SSD_IN
    wg, bg, gnw = gla_args
    conv_w, conv_b, dtb, alog, expand, dsk, nw = ssd_args
    return pl.pallas_call(
        functools.partial(_mixers_multi_kernel, seg_len=seq_rows),
        grid=(m // block_rows, GLA_HEADS // nh),
        in_specs=gla_specs + ssd_specs,
        out_specs=[pl.BlockSpec((block_rows, vw), rmap(0)), s_spec, pl.BlockSpec((block_rows, xw), rmap(0)), h_spec],
        out_shape=[jax.ShapeDtypeStruct((m, GLA_WIDTH), BF16),
                   jax.ShapeDtypeStruct((n_seq, GLA_HEADS, GLA_DK, GLA_DV), F32),
                   jax.ShapeDtypeStruct((m, SSD_WIDTH), BF16),
                   jax.ShapeDtypeStruct((n_seq, SSD_HEADS, SSD_HEAD_DIM, D_STATE), F32)],
        scratch_shapes=[pltpu.VMEM((8 * per + 8, cw), F32), pltpu.VMEM((8 * per, cw), F32),
                        pltpu.VMEM((block_rows, cw), F32)],
        compiler_params=pltpu.CompilerParams(dimension_semantics=("parallel", "parallel"),
                                             vmem_limit_bytes=VMEM_LIMIT),
        name="mixers_multi",
    )(ua, ua, ua, ua, usm, wg, bg, gnw, s0,
      ub, ub, ub, ub, usm, conv_w, conv_w, conv_w, conv_b, conv_b, conv_b, dtb, alog, expand, dsk, nw,
      conv0, conv0, conv0, h0)


def _matmul_kernel(x_ref, w_ref, o_ref, *wb_ref, act, w_t):
    w = w_ref[...].astype(BF16)
    if wb_ref:
        wb_ref[0][...] = w
    acc = _nt(x_ref[...], w) if w_t else _mm(x_ref[...], w)
    if act == "relu2":
        acc = jnp.square(jnp.maximum(acc, 0.0))
    o_ref[...] = acc.astype(o_ref.dtype)


def _matmul(x, w, *, tm, tn, out_dtype, act=None, name, n=None, emit=False, w_t=False, row0=0):
    m, k = x.shape
    n = w.shape[0 if w_t else 1] if n is None else n
    assert not emit or m == tm
    assert row0 == 0 or (w_t and row0 % 16 == 0)
    blk_spec = pl.BlockSpec((tn, k), lambda i, j: (j, 0)) if w_t else pl.BlockSpec((k, tn), lambda i, j: (0, j))
    w_spec = blk_spec if row0 == 0 else pl.BlockSpec((pl.Element(tn), pl.Element(k)),
                                                     lambda i, j: (pl.multiple_of(row0 + j * tn, 16), 0))
    out_specs = [pl.BlockSpec((tm, tn), lambda i, j: (i, j))]
    out_shape = [jax.ShapeDtypeStruct((m, n), out_dtype)]
    if emit:
        out_specs.append(blk_spec)
        out_shape.append(jax.ShapeDtypeStruct((n, k) if w_t else (k, n), BF16))
    res = pl.pallas_call(
        functools.partial(_matmul_kernel, act=act, w_t=w_t),
        grid=(m // tm, n // tn),
        in_specs=[pl.BlockSpec((tm, k), lambda i, j: (i, 0)), w_spec],
        out_specs=out_specs, out_shape=out_shape,
        compiler_params=pltpu.CompilerParams(dimension_semantics=("parallel", "parallel"),
                                             vmem_limit_bytes=VMEM_LIMIT),
        name=name,
    )(x, w)
    return res if emit else res[0]


def _small_proj_kernel(x_ref, w_ref, o_ref, xb_ref):
    xb = x_ref[...].astype(BF16)
    xb_ref[...] = xb
    o_ref[...] = _nt(xb, w_ref[...].astype(BF16))


def _small_proj(x, w, *, tm):
    m, k = x.shape
    return pl.pallas_call(
        _small_proj_kernel,
        grid=(m // tm,),
        in_specs=[pl.BlockSpec((tm, k), lambda i: (i, 0)), pl.BlockSpec((LANE, k), lambda i: (0, 0))],
        out_specs=[pl.BlockSpec((tm, LANE), lambda i: (i, 0)), pl.BlockSpec((tm, k), lambda i: (i, 0))],
        out_shape=[jax.ShapeDtypeStruct((m, LANE), F32), jax.ShapeDtypeStruct((m, k), BF16)],
        compiler_params=pltpu.CompilerParams(dimension_semantics=("parallel",), vmem_limit_bytes=VMEM_LIMIT),
        name="in_proj_small",
    )(x, w)


def _out_proj_kernel(a_ref, b_ref, wa_ref, wb_ref, res_ref, o_ref, *wo_refs):
    wa, wb = wa_ref[...], wb_ref[...]
    if wo_refs:
        wa, wb = wa.astype(BF16), wb.astype(BF16)
        wo_refs[0][...] = wa
        wo_refs[1][...] = wb
    o_ref[...] = ALPHA * res_ref[...] + _mm(a_ref[...], wa) + _mm(b_ref[...], wb)


def _out_proj(a, b, w_top, w_bot, res, *, tm, tn, emit=False):
    m, kh = a.shape
    n = w_top.shape[1]
    assert not emit or m == tm
    out_specs = [pl.BlockSpec((tm, tn), lambda i, j: (i, j))]
    out_shape = [jax.ShapeDtypeStruct((m, n), F32)]
    if emit:
        out_specs += [pl.BlockSpec((kh, tn), lambda i, j: (0, j))] * 2
        out_shape += [jax.ShapeDtypeStruct((kh, n), BF16)] * 2
    res_ = pl.pallas_call(
        _out_proj_kernel,
        grid=(m // tm, n // tn),
        in_specs=[pl.BlockSpec((tm, kh), lambda i, j: (i, 0)), pl.BlockSpec((tm, kh), lambda i, j: (i, 0)),
                  pl.BlockSpec((kh, tn), lambda i, j: (0, j)),
                  pl.BlockSpec((kh, tn), lambda i, j: (1 if emit else 0, j)),
                  pl.BlockSpec((tm, tn), lambda i, j: (i, j))],
        out_specs=out_specs, out_shape=out_shape,
        compiler_params=pltpu.CompilerParams(dimension_semantics=("parallel", "parallel"),
                                             vmem_limit_bytes=VMEM_LIMIT),
        name="out_proj",
    )(a, b, w_top, w_bot, res)
    return res_ if emit else res_[0]


def _down_proj_kernel(x_ref, w_ref, pre_ref, st_ref, g_ref, b_ref, o_ref, *rest):
    acc_ref = rest[-1]
    kk = pl.program_id(2)

    @pl.when(kk == 0)
    def _():
        st = st_ref[...]
        h = (pre_ref[...] - st[:, 0:1]) * st[:, 1:2] * g_ref[...] + b_ref[...]
        acc_ref[...] = ALPHA * h

    w = w_ref[...].astype(BF16)
    if len(rest) == 2:
        rest[0][...] = w
    acc_ref[...] += _mm(x_ref[...], w)

    @pl.when(kk == pl.num_programs(2) - 1)
    def _():
        o_ref[...] = acc_ref[...]


def _down_proj(x, w, pre, stats, g, b, *, tm, tn, tk, emit=False):
    m, k = x.shape
    n = w.shape[1]
    assert not emit or m == tm
    out_specs = [pl.BlockSpec((tm, tn), lambda i, j, kk: (i, j))]
    out_shape = [jax.ShapeDtypeStruct((m, n), F32)]
    if emit:
        out_specs.append(pl.BlockSpec((tk, tn), lambda i, j, kk: (kk, j)))
        out_shape.append(jax.ShapeDtypeStruct((k, n), BF16))
    res_ = pl.pallas_call(
        _down_proj_kernel,
        grid=(m // tm, n // tn, k // tk),
        in_specs=[pl.BlockSpec((tm, tk), lambda i, j, kk: (i, kk)), pl.BlockSpec((tk, tn), lambda i, j, kk: (kk, j)),
                  pl.BlockSpec((tm, tn), lambda i, j, kk: (i, j)), pl.BlockSpec((tm, LANE), lambda i, j, kk: (i, 0)),
                  pl.BlockSpec((1, tn), lambda i, j, kk: (0, j)), pl.BlockSpec((1, tn), lambda i, j, kk: (0, j))],
        out_specs=out_specs, out_shape=out_shape,
        scratch_shapes=[pltpu.VMEM((tm, tn), F32)],
        compiler_params=pltpu.CompilerParams(dimension_semantics=("parallel", "parallel", "arbitrary"),
                                             vmem_limit_bytes=VMEM_LIMIT),
        name="down_proj",
    )(x, w, pre, stats, g, b)
    return res_ if emit else res_[0]


def _layer_norm_kernel(x_ref, g_ref, b_ref, o_ref, *st_ref):
    x = x_ref[...]
    mu = jnp.mean(x, axis=-1, keepdims=True)
    xc = x - mu
    var = jnp.mean(jnp.square(xc), axis=-1, keepdims=True)
    rstd = lax.rsqrt(var + EPS)
    o_ref[...] = (xc * rstd * g_ref[...] + b_ref[...]).astype(o_ref.dtype)
    if st_ref:
        lane = lax.broadcasted_iota(jnp.int32, st_ref[0].shape, 1)
        st_ref[0][...] = jnp.where(lane == 0, mu, jnp.where(lane == 1, rstd, 0.0))


def _layer_norm(x, g, b, *, tm, dtype, stats=False):
    m, d = x.shape
    out_specs = [pl.BlockSpec((tm, d), lambda i: (i, 0))]
    out_shape = [jax.ShapeDtypeStruct((m, d), dtype)]
    if stats:
        out_specs.append(pl.BlockSpec((tm, LANE), lambda i: (i, 0)))
        out_shape.append(jax.ShapeDtypeStruct((m, LANE), F32))
    res = pl.pallas_call(
        _layer_norm_kernel,
        grid=(m // tm,),
        in_specs=[pl.BlockSpec((tm, d), lambda i: (i, 0)), pl.BlockSpec((1, d), lambda i: (0, 0)),
                  pl.BlockSpec((1, d), lambda i: (0, 0))],
        out_specs=out_specs, out_shape=out_shape,
        compiler_params=pltpu.CompilerParams(dimension_semantics=("parallel",), vmem_limit_bytes=VMEM_LIMIT),
        name="layer_norm",
    )(x, g, b)
    return res if stats else res[0]


def _expand_matrix():
    e = np.zeros((SSD_GROUPS, LANE, GROUP_WIDTH), np.float32)
    for g in range(SSD_GROUPS):
        for j in range(HEADS_PER_GROUP):
            e[g, SM_DT + g * HEADS_PER_GROUP + j, j * SSD_HEAD_DIM:(j + 1) * SSD_HEAD_DIM] = 1.0
    return jnp.asarray(e, BF16)


def _lane_param(p):
    return jnp.zeros((1, LANE), F32).at[0, SM_DT:SM_DT + SSD_HEADS].set(p.astype(F32))


def _dense_tail(x, og, ys, prm, tm, emit):
    wb = dict(prm)
    tn = 512 if emit else 1024
    w_top, w_bot = (prm["w_out"], prm["w_out"]) if emit else prm["w_out"]
    pre1 = _out_proj(og, ys, w_top, w_bot, x, tm=tm, tn=tn, emit=emit)
    if emit:
        pre1, wt, wbt = pre1
        wb["w_out"] = (wt, wbt)
    hb, stats = _layer_norm(pre1, prm["ln1_g"], prm["ln1_b"], tm=min(tm, 256), dtype=BF16, stats=True)
    hid = _matmul(hb, prm["w_up"], tm=tm, tn=tn, out_dtype=BF16, act="relu2", name="mlp_up", emit=emit)
    if emit:
        hid, wb["w_up"] = hid
    pre2 = _down_proj(hid, prm["w_down"], pre1, stats, prm["ln1_g"], prm["ln1_b"], tm=tm, tn=1024,
                      tk=2048, emit=emit)
    if emit:
        pre2, wb["w_down"] = pre2
    out = _layer_norm(pre2, prm["ln2_g"], prm["ln2_b"], tm=min(tm, 256), dtype=F32)
    return out, wb


def kernel(x_prompt, x_sample, state_gla, state_ssm, state_conv, meta_tokens, w_in, w_gk_up, b_gk, gla_norm_w,
           conv_w, conv_b, dt_bias, a_log, d_skip, ssd_norm_w, w_out, ln1_g, ln1_b, w_up, w_down, ln2_g, ln2_b):
    bp, seq, d = x_prompt.shape
    bs, dec = x_sample.shape[:2]
    assert w_in.shape[0] == DEPTH and seq % CHUNK == 0 and dec == 4 and N_META <= CHUNK
    assert bs % SEQS_PER_BLOCK == 0
    wi = w_in[0]
    o_g = UA_WIDTH
    o_z = o_g + GATE_RANK
    o_dt = o_z + UB_WIDTH
    wt = wi.T
    w_small = jnp.concatenate([wt[o_g:o_z], wt[o_dt:], jnp.zeros((LANE - GATE_RANK - SSD_HEADS, d), F32)], axis=0)
    wg = jnp.concatenate([w_gk_up[0], jnp.zeros((LANE - GATE_RANK, GLA_QK), F32)], axis=0).astype(BF16)
    prm = dict(
        w_out=w_out[0], w_up=w_up[0], w_down=w_down[0],
        ln1_g=ln1_g[0][None], ln1_b=ln1_b[0][None], ln2_g=ln2_g[0][None], ln2_b=ln2_b[0][None],
    )
    gla_args = (wg, b_gk[0][None], gla_norm_w[0][None])
    ssd_args = (conv_w[0], conv_b[0][None], _lane_param(dt_bias[0]), _lane_param(a_log[0]), _expand_matrix(),
                jnp.repeat(d_skip[0], SSD_HEAD_DIM)[None, :], ssd_norm_w[0][None])
    tail = CONV_W - 1

    def in_proj(x, tm, w_ab):
        usm, xb = _small_proj(x, w_small, tm=512 if tm % 512 == 0 else tm)
        if w_ab is None:
            ua, w_a = _matmul(xb, wt, tm=tm, tn=512, out_dtype=F32, name="in_proj_gla", n=UA_WIDTH, emit=True,
                              w_t=True)
            ub, w_b = _matmul(xb, wt, tm=tm, tn=512, out_dtype=F32, name="in_proj_ssd", n=UB_WIDTH, emit=True,
                              w_t=True, row0=o_z)
        else:
            w_a, w_b = w_ab
            ua = _matmul(xb, w_a, tm=tm, tn=1024, out_dtype=F32, name="in_proj_gla", w_t=True)
            ub = _matmul(xb, w_b, tm=tm, tn=1024, out_dtype=F32, name="in_proj_ssd", w_t=True)
        return ua, ub, usm, (w_a, w_b)

    ns = bs * dec
    lead_null = CHUNK - N_META
    xs = x_sample.reshape(ns, d)
    xsm = jnp.concatenate([xs, jnp.zeros((lead_null, d), F32), meta_tokens], axis=0)
    ua_s, ub_s, usm_s, w_ab = in_proj(xsm, ns + CHUNK, None)
    og_s, gla_s, ys_s, ssm_s = _mixers_multi(ua_s, ub_s, usm_s, gla_args, ssd_args, state_gla[0], state_conv[0],
                                             state_ssm[0], n_seq=bs, seq_rows=dec)
    y_sample, prm_b = _dense_tail(xs, og_s, ys_s, prm, ns, True)
    y_sample = y_sample.reshape(bs, dec, d)
    conv_s = ub_s[:ns, UB_X:].reshape(bs, dec, CONV_DIM)[:, dec - tail:]

    _, gla_m = _gla_scan(ua_s, usm_s, *gla_args, jnp.zeros((1, GLA_HEADS, GLA_DK, GLA_DV), F32),
                         n_seq=1, seq_rows=CHUNK, block_rows=CHUNK, carry=True, lead_null=lead_null, row0=ns)
    _, ssm_m = _ssd_scan(ub_s, usm_s, *ssd_args, jnp.zeros((1, 8, CONV_DIM), F32),
                         jnp.zeros((1, SSD_HEADS, SSD_HEAD_DIM, D_STATE), F32),
                         n_seq=1, seq_rows=CHUNK, block_rows=CHUNK, carry=True, lead_null=lead_null, row0=ns)
    conv_m = ub_s[None, ns + CHUNK - 8:, UB_X:]

    xp = x_prompt.reshape(bp * seq, d)
    ua_p, ub_p, usm_p, _ = in_proj(xp, 1024, w_ab)
    og_p, gla_p = _gla_scan(ua_p, usm_p, *gla_args, gla_m, n_seq=bp, seq_rows=seq, block_rows=256, carry=True)
    ys_p, ssm_p = _ssd_scan(ub_p, usm_p, *ssd_args, conv_m, ssm_m, n_seq=bp, seq_rows=seq, block_rows=256,
                            carry=True)
    y_prompt, _ = _dense_tail(xp, og_p, ys_p, prm_b, 1024, False)
    y_prompt = y_prompt.reshape(bp, seq, d)
    conv_p = jnp.stack([ub_p[(b + 1) * seq - tail:(b + 1) * seq, UB_X:] for b in range(bp)])

    return (y_prompt, y_sample, gla_p[None], ssm_p[None], conv_p[None], gla_s[None], ssm_s[None], conv_s[None])
```

```python
import functools

import numpy as np
import jax
import jax.numpy as jnp
from jax import lax
from jax.experimental import pallas as pl
from jax.experimental.pallas import tpu as pltpu

F32 = jnp.float32
BF16 = jnp.bfloat16

GLA_HEADS = 4
GLA_DK = 256
GLA_DV = 512
GLA_QK = GLA_HEADS * GLA_DK
GLA_WIDTH = GLA_HEADS * GLA_DV
GATE_RANK = 16
GATE_NORM = 16.0
SSD_GROUPS = 8
HEADS_PER_GROUP = 4
SSD_HEAD_DIM = 64
SSD_HEADS = SSD_GROUPS * HEADS_PER_GROUP
GROUP_WIDTH = HEADS_PER_GROUP * SSD_HEAD_DIM
SSD_WIDTH = SSD_GROUPS * GROUP_WIDTH
D_STATE = 128
BC_WIDTH = SSD_GROUPS * D_STATE
CONV_W = 4
CONV_DIM = SSD_WIDTH + 2 * BC_WIDTH
N_META = 16
CHUNK = 64
CHUNK_SHIFT = 6
EPS = 1e-5
DEPTH = 1
ALPHA = (2.0 * DEPTH) ** 0.25

UA_K = GLA_QK
UA_V = 2 * GLA_QK
UA_R = UA_V + GLA_WIDTH
UA_WIDTH = UA_R + GLA_WIDTH
UB_X = SSD_WIDTH
UB_B = UB_X + SSD_WIDTH
UB_C = UB_B + BC_WIDTH
UB_WIDTH = UB_C + BC_WIDTH
LANE = 128
SM_DT = GATE_RANK

GLA_HEADS_PER_STEP = 4
SSD_GROUPS_PER_STEP = 4
SEQS_PER_BLOCK = 4

VMEM_LIMIT = 56 * 1024 * 1024


def _mm(a, b):
    return jnp.dot(a, b, preferred_element_type=F32)


def _nt(a, b):
    return lax.dot_general(a, b, (((1,), (1,)), ((), ())), preferred_element_type=F32)


def _split2(x):
    hi = x.astype(BF16)
    lo = (x - hi.astype(F32)).astype(BF16)
    return hi, lo


def _mm_split(m01, x):
    hi, lo = _split2(x)
    return _mm(m01, hi) + _mm(m01, lo)


def _pad_rows(x, n):
    t = x.shape[0]
    if t >= n:
        return x
    return jnp.concatenate([x, jnp.zeros((n - t, x.shape[1]), x.dtype)], axis=0)


def _silu(x):
    return x * (1.0 / (1.0 + jnp.exp(-x)))


def _softplus(x):
    return jnp.maximum(x, 0.0) + jnp.log(1.0 + jnp.exp(-jnp.abs(x)))


def _f01(m):
    return jnp.where(m, 1.0, 0.0).astype(BF16)


def _eye_bf16(n):
    ri = lax.broadcasted_iota(jnp.int32, (n, n), 0)
    ci = lax.broadcasted_iota(jnp.int32, (n, n), 1)
    return _f01(ri == ci)


def _split3_rows(x, period):
    hi = x.astype(BF16)
    r1 = x - hi.astype(F32)
    mid = r1.astype(BF16)
    lo = (r1 - mid.astype(F32)).astype(BF16)
    ridx = lax.broadcasted_iota(jnp.int32, x.shape, 0) & (period - 1)
    out = jnp.where(ridx == 0, hi.astype(F32),
                    jnp.where(ridx == 1, mid.astype(F32),
                              jnp.where(ridx == 2, lo.astype(F32), 0.0)))
    return out.astype(BF16)


def _lane_group_sum(t, shift, idx):
    lane = lax.broadcasted_iota(jnp.int32, t.shape, 1)
    return jnp.sum(jnp.where((lane >> shift) == idx, t, 0.0), axis=1, keepdims=True)


def _chunk_rows_only(xb, c):
    rows_p, w = xb.shape
    lo, hi = c * CHUNK, (c + 1) * CHUNK
    pieces = []
    if lo:
        pieces.append(jnp.zeros((lo, w), xb.dtype))
    pieces.append(xb[lo:hi])
    if hi < rows_p:
        pieces.append(jnp.zeros((rows_p - hi, w), xb.dtype))
    return pieces[0] if len(pieces) == 1 else jnp.concatenate(pieces, axis=0)


def _lane_group_keep(t, shift, idx):
    lane = lax.broadcasted_iota(jnp.int32, t.shape, 1)
    return jnp.where((lane >> shift) == idx, t, 0.0).astype(BF16)


def _chunk_masks(rows, cols, lead_null, lane_mod=None):
    ri = lax.broadcasted_iota(jnp.int32, (rows, cols), 0)
    ci = lax.broadcasted_iota(jnp.int32, (rows, cols), 1)
    if lane_mod is None:
        same = (ri >> CHUNK_SHIFT) == (ci >> CHUNK_SHIFT)
        cl = ci & (CHUNK - 1)
    else:
        cl = ci & lane_mod
        same = None
    rl = ri & (CHUNK - 1)
    valid = cl >= lead_null if lead_null else None
    base = same
    if valid is not None:
        base = valid if base is None else (base & valid)
    causal = (cl <= rl) if base is None else (base & (cl <= rl))
    middle = (lead_null + CHUNK) // 2 - 1
    upto_mid = (cl <= middle) if base is None else (base & (cl <= middle))
    return base, causal, upto_mid


def _seg_masks(tb, cols, seg_len, lane_mod=None):
    ri = lax.broadcasted_iota(jnp.int32, (tb, cols), 0)
    ci = lax.broadcasted_iota(jnp.int32, (tb, cols), 1)
    if lane_mod is not None:
        ci = ci & lane_mod
    shift = seg_len.bit_length() - 1
    same = ((ri >> shift) == (ci >> shift)) & (ci < tb)
    causal = same & (ci <= ri)
    upto_mid = same & ((ci & (seg_len - 1)) <= seg_len // 2 - 1)
    return same, causal, upto_mid


def _gla_gate(sm_b, wg, bg):
    gpre = _mm(sm_b, wg) + bg
    return -_softplus(-gpre) * (1.0 / GATE_NORM)


def _gla_finish(o, gnw, r):
    o = o * lax.rsqrt(jnp.mean(jnp.square(o), axis=-1, keepdims=True) + EPS)
    return o * gnw * _silu(r)


def _gla_block(q, k, v, g, mats, causal_p, eye):
    tri, midm, allm = mats
    rows, dk = q.shape
    rows_p = max(rows, LANE)
    bc = _mm_split(tri, g)
    anchor = _mm_split(midm, g)
    blast = _mm_split(allm, g)
    qs = q * (dk ** -0.5)
    qt = (qs * jnp.exp(bc - anchor)).astype(BF16)
    kt = _pad_rows((k * jnp.exp(anchor - bc)).astype(BF16), rows_p)
    p = jnp.where(causal_p, _nt(qt, kt), 0.0)
    vb = _pad_rows(v.astype(BF16), rows_p)
    o = _mm(p.astype(BF16), vb)
    qb = (qs * jnp.exp(bc)).astype(BF16)
    kd = _pad_rows((k * jnp.exp(blast - bc)).astype(BF16), rows_p)
    kd_t = _nt(eye, kd)
    return o, qb, kd_t, vb, jnp.exp(blast)


def _gla_carry_kernel(q_ref, k_ref, v_ref, r_ref, sm_ref, wg_ref, bg_ref, gnw_ref, s0_ref,
                      og_ref, s_ref, *, lead_null):
    rows = q_ref.shape[0]
    nchunk = rows // CHUNK
    rows_p = max(rows, LANE)

    @pl.when(pl.program_id(2) == 0)
    def _():
        s_ref[...] = s0_ref[...]

    eye = _eye_bf16(GLA_DK)
    same, causal, upto_mid = _chunk_masks(rows, rows, lead_null)
    mats = (_f01(causal), _f01(upto_mid), _f01(same))
    causal_p = _chunk_masks(rows, rows_p, lead_null)[1]
    sm_b = sm_ref[...].astype(BF16)
    for hh in range(GLA_HEADS_PER_STEP):
        ks = slice(hh * GLA_DK, (hh + 1) * GLA_DK)
        vs = slice(hh * GLA_DV, (hh + 1) * GLA_DV)
        g = _gla_gate(sm_b, wg_ref[:, ks], bg_ref[:, ks])
        k = k_ref[:, ks]
        if lead_null:
            rl = lax.broadcasted_iota(jnp.int32, (rows, 1), 0) & (CHUNK - 1)
            k = jnp.where(rl >= lead_null, k, 0.0)
        o, qb, kd_t, vb, e_last = _gla_block(q_ref[:, ks], k, v_ref[:, vs], g, mats, causal_p, eye)
        e_rows = [_split3_rows(e_last[c * CHUNK:c * CHUNK + 16], 16) for c in range(nchunk)]
        e_rows.append(jnp.zeros((LANE - 16 * nchunk, GLA_DK), BF16))
        e_t = _nt(eye, jnp.concatenate(e_rows, axis=0))
        s = s_ref[0, hh]
        kd_tb = kd_t.astype(BF16)
        inter = []
        for c in range(nchunk):
            inter.append(_mm(qb[c * CHUNK:(c + 1) * CHUNK], s.astype(BF16)))
            s = _lane_group_sum(e_t, 4, c) * s + _mm(kd_tb, _chunk_rows_only(vb, c))
        s_ref[0, hh] = s
        o = o + (inter[0] if nchunk == 1 else jnp.concatenate(inter, axis=0))
        og_ref[:, vs] = _gla_finish(o, gnw_ref[...], r_ref[:, vs]).astype(og_ref.dtype)


def _gla_multi_kernel(q_ref, k_ref, v_ref, r_ref, sm_ref, wg_ref, bg_ref, gnw_ref, s0_ref,
                      og_ref, s_ref, *, seg_len, nh=GLA_HEADS_PER_STEP):
    tb = q_ref.shape[0]
    nseq = tb // seg_len
    seg_shift = seg_len.bit_length() - 1
    eye = _eye_bf16(GLA_DK)
    same, causal, upto_mid = _seg_masks(tb, tb, seg_len)
    mats = (_f01(causal), _f01(upto_mid), _f01(same))
    causal_p = _seg_masks(tb, LANE, seg_len)[1]
    sm_b = sm_ref[...].astype(BF16)
    for hh in range(nh):
        ks = slice(hh * GLA_DK, (hh + 1) * GLA_DK)
        vs = slice(hh * GLA_DV, (hh + 1) * GLA_DV)
        g = _gla_gate(sm_b, wg_ref[:, ks], bg_ref[:, ks])
        o, qb, kd_t, vb, e_last = _gla_block(q_ref[:, ks], k_ref[:, ks], v_ref[:, vs], g, mats, causal_p, eye)
        e_rows = jnp.concatenate([_split3_rows(e_last, seg_len), jnp.zeros((LANE - tb, GLA_DK), BF16)], axis=0)
        e_t = _nt(eye, e_rows)
        rowseg = lax.broadcasted_iota(jnp.int32, (tb, 1), 0) >> seg_shift
        for i in range(nseq):
            s = s0_ref[i, hh]
            o = o + jnp.where(rowseg == i, _mm(qb, s.astype(BF16)), 0.0)
            s_ref[i, hh] = (_lane_group_sum(e_t, seg_shift, i) * s
                            + _mm(_lane_group_keep(kd_t, seg_shift, i), vb))
        og_ref[:, vs] = _gla_finish(o, gnw_ref[...], r_ref[:, vs]).astype(og_ref.dtype)


def _gla_scan(ua, usm, wg, bg, gnw, s0, *, n_seq, seq_rows, block_rows, lead_null=0, row0=0):
    m = n_seq * seq_rows
    r0 = row0 // block_rows
    nh = GLA_HEADS_PER_STEP
    kw, vw = nh * GLA_DK, nh * GLA_DV
    nb = seq_rows // block_rows
    grid = (n_seq, GLA_HEADS // nh, nb)
    s0_mul = 1 if s0.shape[0] == n_seq else 0
    rmap = lambda off: (lambda b, h, c: (r0 + b * nb + c, off + h))
    smap = lambda b, h, c: (r0 + b * nb + c, 0)
    omap = lambda b, h, c: (b * nb + c, h)
    cmap = lambda b, h, c: (0, h)
    zmap = lambda b, h, c: (0, 0)
    s0_spec = pl.BlockSpec((1, nh, GLA_DK, GLA_DV), lambda b, h, c: (b * s0_mul, h, 0, 0))
    so_spec = pl.BlockSpec((1, nh, GLA_DK, GLA_DV), lambda b, h, c: (b, h, 0, 0))
    in_specs = [
        pl.BlockSpec((block_rows, kw), rmap(0)), pl.BlockSpec((block_rows, kw), rmap(UA_K // kw)),
        pl.BlockSpec((block_rows, vw), rmap(UA_V // vw)), pl.BlockSpec((block_rows, vw), rmap(UA_R // vw)),
        pl.BlockSpec((block_rows, LANE), smap),
        pl.BlockSpec((LANE, kw), cmap), pl.BlockSpec((1, kw), cmap), pl.BlockSpec((1, GLA_DV), zmap),
        s0_spec,
    ]
    out_specs = [pl.BlockSpec((block_rows, vw), omap), so_spec]
    return pl.pallas_call(
        functools.partial(_gla_carry_kernel, lead_null=lead_null),
        grid=grid, in_specs=in_specs, out_specs=out_specs,
        out_shape=[jax.ShapeDtypeStruct((m, GLA_WIDTH), BF16),
                   jax.ShapeDtypeStruct((n_seq, GLA_HEADS, GLA_DK, GLA_DV), F32)],
        compiler_params=pltpu.CompilerParams(dimension_semantics=("parallel", "parallel", "arbitrary"),
                                             vmem_limit_bytes=VMEM_LIMIT),
        name="gla_scan_carry",
    )(ua, ua, ua, ua, usm, wg, bg, gnw, s0)


def _ssd_block(x, dt, dta, expand, tri, allm, causal_l, eye, row_valid):
    rows, gw = x.shape
    rows_p = max(rows, LANE)
    p_hi, p_lo = [_mm(p, expand).astype(BF16) for p in _split2(dta)]
    cb = _mm(tri, p_hi) + _mm(tri, p_lo)
    cl = _mm(allm, p_hi) + _mm(allm, p_lo)
    ri = lax.broadcasted_iota(jnp.int32, (rows, gw), 0)
    li = lax.broadcasted_iota(jnp.int32, (rows, gw), 1)
    diag = jnp.where((li & (CHUNK - 1)) == (ri & (CHUNK - 1)), cb, 0.0)
    rw = _mm_split(allm, diag)
    decay = jnp.exp(jnp.where(causal_l, cb - rw, -jnp.inf))
    dt_hi, dt_lo = _split2(dt)
    xdt = x * (_mm(dt_hi, expand) + _mm(dt_lo, expand))
    if row_valid is not None:
        xdt = jnp.where(row_valid, xdt, 0.0)
    xw_t = _nt(eye, _pad_rows((xdt * jnp.exp(cl - cb)).astype(BF16), rows_p))
    return decay, xdt, jnp.exp(cb), xw_t, jnp.exp(cl)


def _ssd_intra(cm_b, bm_b, decay, xdt, seg_rows):
    gw = xdt.shape[1]
    b64 = _pad_rows(bm_b, CHUNK)
    cb4 = _nt(cm_b, jnp.concatenate([b64] * HEADS_PER_GROUP, axis=0))
    x64 = _pad_rows(xdt, CHUNK)
    r2 = lax.broadcasted_iota(jnp.int32, (gw, gw), 0)
    l2 = lax.broadcasted_iota(jnp.int32, (gw, gw), 1)
    bd = jnp.where((r2 >> CHUNK_SHIFT) == (l2 >> CHUNK_SHIFT),
                   jnp.concatenate([x64] * HEADS_PER_GROUP, axis=0), 0.0)
    return _mm((cb4 * decay).astype(BF16), bd.astype(BF16))


def _conv_rows(xpad, cw_refs, cb_refs, n):
    w = [jnp.concatenate([r[pl.ds(i, 1), :] for r in cw_refs], axis=1) for i in range(CONV_W)]
    bias = jnp.concatenate([r[...] for r in cb_refs], axis=1)
    xa = xpad[pl.ds(0, n + 8), :]
    x1 = pltpu.roll(xa, 1, axis=0)
    a = w[3] * xa + w[2] * x1
    b = w[1] * xa + w[0] * x1
    y = a + pltpu.roll(b, 2, axis=0) + bias
    return y[8:]


def _ssd_finish(y, x, dsk, z, nw):
    y = y + dsk * x
    y = y * _silu(z)
    y = y * lax.rsqrt(jnp.mean(jnp.square(y), axis=-1, keepdims=True) + EPS)
    return y * nw


def _ssd_lanes():
    ng = SSD_GROUPS_PER_STEP
    xw, sw = ng * GROUP_WIDTH, ng * D_STATE
    return ((0, xw), (xw, xw + sw), (xw + sw, xw + 2 * sw))


def _ssd_carry_kernel(x_ref, b_ref, c_ref, z_ref, sm_ref, cwx_ref, cwb_ref, cwc_ref, cbx_ref, cbb_ref, cbc_ref,
                      dtb_ref, alog_ref, ex_ref, dsk_ref, nw_ref, csx_ref, csb_ref, csc_ref, h0_ref,
                      ys_ref, h_ref, xpad, *, lead_null):
    rows = x_ref.shape[0]
    nchunk = rows // CHUNK
    rows_p = max(rows, LANE)
    gw = GROUP_WIDTH
    lanes = _ssd_lanes()

    @pl.when(pl.program_id(2) == 0)
    def _():
        h_ref[...] = h0_ref[...]
        for (lo, hi), cs in zip(lanes, (csx_ref, csb_ref, csc_ref)):
            xpad[pl.ds(0, 8), lo:hi] = cs[0]

    for (lo, hi), src in zip(lanes, (x_ref, b_ref, c_ref)):
        xpad[pl.ds(8, rows), lo:hi] = src[...]
    xc = _silu(_conv_rows(xpad, (cwx_ref, cwb_ref, cwc_ref), (cbx_ref, cbb_ref, cbc_ref), rows))
    xpad[pl.ds(0, 8), :] = xpad[pl.ds(rows, 8), :]

    eye = _eye_bf16(gw)
    same, causal, _ = _chunk_masks(rows, rows, lead_null)
    tri, allm = _f01(causal), _f01(same)
    causal_l = _chunk_masks(rows, gw, lead_null, lane_mod=CHUNK - 1)[1]
    row_valid = None
    if lead_null:
        row_valid = (lax.broadcasted_iota(jnp.int32, (rows, 1), 0) & (CHUNK - 1)) >= lead_null
    dt = _softplus(sm_ref[...] + dtb_ref[...])
    dta = dt * (-jnp.exp(alog_ref[...]))
    for gg in range(SSD_GROUPS_PER_STEP):
        xs = slice(gg * gw, (gg + 1) * gw)
        x = xc[:, lanes[0][0] + gg * gw:lanes[0][0] + (gg + 1) * gw]
        bm_b = xc[:, lanes[1][0] + gg * D_STATE:lanes[1][0] + (gg + 1) * D_STATE].astype(BF16)
        cm_b = xc[:, lanes[2][0] + gg * D_STATE:lanes[2][0] + (gg + 1) * D_STATE].astype(BF16)
        decay, xdt, ecb, xw_t, e_last = _ssd_block(x, dt, dta, ex_ref[gg], tri, allm, causal_l, eye, row_valid)
        bpad = _pad_rows(bm_b, rows_p)
        e_rows = [_split3_rows(e_last[c * CHUNK:c * CHUNK + 16], 16) for c in range(nchunk)]
        e_rows.append(jnp.zeros((LANE - 16 * nchunk, gw), BF16))
        e_t = _nt(eye, jnp.concatenate(e_rows, axis=0))
        hs = slice(gg * HEADS_PER_GROUP, (gg + 1) * HEADS_PER_GROUP)
        h = h_ref[0, hs].reshape(gw, D_STATE)
        xw_tb = xw_t.astype(BF16)
        ys = []
        for c in range(nchunk):
            rs = slice(c * CHUNK, (c + 1) * CHUNK)
            y = _ssd_intra(cm_b[rs], bm_b[rs], decay[rs], xdt[rs], CHUNK)
            ys.append(y + _nt(cm_b[rs], h.astype(BF16)) * ecb[rs])
            h = _lane_group_sum(e_t, 4, c) * h + _mm(xw_tb, _chunk_rows_only(bpad, c))
        h_ref[0, hs] = h.reshape(HEADS_PER_GROUP, SSD_HEAD_DIM, D_STATE)
        y = ys[0] if nchunk == 1 else jnp.concatenate(ys, axis=0)
        ys_ref[:, xs] = _ssd_finish(y, x, dsk_ref[:, xs], z_ref[:, xs], nw_ref[:, xs]).astype(ys_ref.dtype)


def _ssd_multi_kernel(x_ref, b_ref, c_ref, z_ref, sm_ref, cwx_ref, cwb_ref, cwc_ref, cbx_ref, cbb_ref, cbc_ref,
                      dtb_ref, alog_ref, ex_ref, dsk_ref, nw_ref, csx_ref, csb_ref, csc_ref, h0_ref,
                      ys_ref, h_ref, xpad, cres, xcs, *, seg_len):
    tb = x_ref.shape[0]
    nseq = tb // seg_len
    seg_shift = seg_len.bit_length() - 1
    gw = GROUP_WIDTH
    lanes = _ssd_lanes()
    xpad[...] = jnp.zeros(xpad.shape, F32)
    for i in range(nseq):
        for (lo, hi), src, cs in zip(lanes, (x_ref, b_ref, c_ref), (csx_ref, csb_ref, csc_ref)):
            xpad[pl.ds(8 + 8 * i + 1, CONV_W - 1), lo:hi] = cs[i]
            xpad[pl.ds(8 + 8 * i + 4, seg_len), lo:hi] = src[pl.ds(seg_len * i, seg_len), :]
    cres[...] = _conv_rows(xpad, (cwx_ref, cwb_ref, cwc_ref), (cbx_ref, cbb_ref, cbc_ref), 8 * nseq)
    for i in range(nseq):
        xcs[pl.ds(seg_len * i, seg_len), :] = cres[pl.ds(8 * i + 4, seg_len), :]
    xc = _silu(xcs[...])

    eye = _eye_bf16(gw)
    same, causal, _ = _seg_masks(tb, tb, seg_len)
    tri, allm = _f01(causal), _f01(same)
    causal_l = _seg_masks(tb, gw, seg_len, lane_mod=CHUNK - 1)[1]
    dt = _softplus(sm_ref[...] + dtb_ref[...])
    dta = dt * (-jnp.exp(alog_ref[...]))
    rowseg = lax.broadcasted_iota(jnp.int32, (tb, 1), 0) >> seg_shift
    for gg in range(SSD_GROUPS_PER_STEP):
        xs = slice(gg * gw, (gg + 1) * gw)
        x = xc[:, lanes[0][0] + gg * gw:lanes[0][0] + (gg + 1) * gw]
        bm_b = xc[:, lanes[1][0] + gg * D_STATE:lanes[1][0] + (gg + 1) * D_STATE].astype(BF16)
        cm_b = xc[:, lanes[2][0] + gg * D_STATE:lanes[2][0] + (gg + 1) * D_STATE].astype(BF16)
        decay, xdt, ecb, xw_t, e_last = _ssd_block(x, dt, dta, ex_ref[gg], tri, allm, causal_l, eye, None)
        bpad = _pad_rows(bm_b, LANE)
        e_rows = jnp.concatenate([_split3_rows(e_last, seg_len), jnp.zeros((LANE - tb, gw), BF16)], axis=0)
        e_t = _nt(eye, e_rows)
        y = _ssd_intra(cm_b, bm_b, decay, xdt, tb)
        hs = slice(gg * HEADS_PER_GROUP, (gg + 1) * HEADS_PER_GROUP)
        for i in range(nseq):
            h = h0_ref[i, hs].reshape(gw, D_STATE)
            y = y + jnp.where(rowseg == i, _nt(cm_b, h.astype(BF16)) * ecb, 0.0)
            h = _lane_group_sum(e_t, seg_shift, i) * h + _mm(_lane_group_keep(xw_t, seg_shift, i), bpad)
            h_ref[i, hs] = h.reshape(HEADS_PER_GROUP, SSD_HEAD_DIM, D_STATE)
        ys_ref[:, xs] = _ssd_finish(y, x, dsk_ref[:, xs], z_ref[:, xs], nw_ref[:, xs]).astype(ys_ref.dtype)


def _ssd_scan(ub, usm, conv_w, conv_b, dtb, alog, expand, dsk, nw, conv0, h0, *,
              n_seq, seq_rows, block_rows, lead_null=0, row0=0):
    m = n_seq * seq_rows
    r0 = row0 // block_rows
    ng = SSD_GROUPS_PER_STEP
    xw, sw = ng * GROUP_WIDTH, ng * D_STATE
    cw = xw + 2 * sw
    hpg = ng * HEADS_PER_GROUP
    nb = seq_rows // block_rows
    grid = (n_seq, SSD_GROUPS // ng, nb)
    c_mul = 1 if conv0.shape[0] == n_seq else 0
    h_mul = 1 if h0.shape[0] == n_seq else 0
    rmap = lambda off: (lambda b, g, c: (r0 + b * nb + c, off + g))
    smap = lambda b, g, c: (r0 + b * nb + c, 0)
    omap = lambda b, g, c: (b * nb + c, g)
    wmap = lambda off: (lambda b, g, c: (0, off + g))
    zmap = lambda b, g, c: (0, 0)
    exmap = lambda b, g, c: (g, 0, 0)
    cs_spec = lambda width, off: pl.BlockSpec((1, 8, width), lambda b, g, c: (b * c_mul, 0, off + g))
    h0_spec = pl.BlockSpec((1, hpg, SSD_HEAD_DIM, D_STATE), lambda b, g, c: (b * h_mul, g, 0, 0))
    ho_spec = pl.BlockSpec((1, hpg, SSD_HEAD_DIM, D_STATE), lambda b, g, c: (b, g, 0, 0))
    cvb, cvc = SSD_WIDTH // sw, (SSD_WIDTH + BC_WIDTH) // sw
    in_specs = [
        pl.BlockSpec((block_rows, xw), rmap(UB_X // xw)), pl.BlockSpec((block_rows, sw), rmap(UB_B // sw)),
        pl.BlockSpec((block_rows, sw), rmap(UB_C // sw)), pl.BlockSpec((block_rows, xw), rmap(0)),
        pl.BlockSpec((block_rows, LANE), smap),
        pl.BlockSpec((CONV_W, xw), wmap(0)), pl.BlockSpec((CONV_W, sw), wmap(cvb)), pl.BlockSpec((CONV_W, sw), wmap(cvc)),
        pl.BlockSpec((1, xw), wmap(0)), pl.BlockSpec((1, sw), wmap(cvb)), pl.BlockSpec((1, sw), wmap(cvc)),
        pl.BlockSpec((1, LANE), zmap), pl.BlockSpec((1, LANE), zmap),
        pl.BlockSpec((ng, LANE, GROUP_WIDTH), exmap),
        pl.BlockSpec((1, xw), wmap(0)), pl.BlockSpec((1, xw), wmap(0)),
        cs_spec(xw, 0), cs_spec(sw, cvb), cs_spec(sw, cvc),
        h0_spec,
    ]
    out_specs = [pl.BlockSpec((block_rows, xw), omap), ho_spec]
    return pl.pallas_call(
        functools.partial(_ssd_carry_kernel, lead_null=lead_null),
        grid=grid, in_specs=in_specs, out_specs=out_specs,
        out_shape=[jax.ShapeDtypeStruct((m, SSD_WIDTH), BF16),
                   jax.ShapeDtypeStruct((n_seq, SSD_HEADS, SSD_HEAD_DIM, D_STATE), F32)],
        scratch_shapes=[pltpu.VMEM((block_rows + 8, cw), F32)],
        compiler_params=pltpu.CompilerParams(dimension_semantics=("parallel", "parallel", "arbitrary"),
                                             vmem_limit_bytes=VMEM_LIMIT),
        name="ssd_scan_carry",
    )(ub, ub, ub, ub, usm, conv_w, conv_w, conv_w, conv_b, conv_b, conv_b, dtb, alog, expand, dsk, nw,
      conv0, conv0, conv0, h0)


N_GLA_IN = 9
N_SSD_IN = 20
MULTI_GLA_HEADS = 2


def _mixers_multi_kernel(*refs, seg_len):
    gla_in = refs[:N_GLA_IN]
    ssd_in = refs[N_GLA_IN:N_GLA_IN + N_SSD_IN]
    og_ref, s_ref, ys_ref, h_ref, xpad, cres, xcs = refs[N_GLA_IN + N_SSD_IN:]
    _gla_multi_kernel(*gla_in, og_ref, s_ref, seg_len=seg_len, nh=MULTI_GLA_HEADS)
    _ssd_multi_kernel(*ssd_in, ys_ref, h_ref, xpad, cres, xcs, seg_len=seg_len)


def _mixers_multi(ua, ub, usm, gla_args, ssd_args, s0, conv0, h0, *, n_seq, seq_rows):
    per = SEQS_PER_BLOCK
    block_rows = per * seq_rows
    m = n_seq * seq_rows
    nh, ng = MULTI_GLA_HEADS, SSD_GROUPS_PER_STEP
    assert GLA_HEADS // nh == SSD_GROUPS // ng
    kw, vw = nh * GLA_DK, nh * GLA_DV
    xw, sw = ng * GROUP_WIDTH, ng * D_STATE
    cw = xw + 2 * sw
    hpg = ng * HEADS_PER_GROUP
    cvb, cvc = SSD_WIDTH // sw, (SSD_WIDTH + BC_WIDTH) // sw
    rmap = lambda off: (lambda i, p: (i, off + p))
    smap = lambda i, p: (i, 0)
    wmap = lambda off: (lambda i, p: (0, off + p))
    zmap = lambda i, p: (0, 0)
    cs_spec = lambda width, off: pl.BlockSpec((per, CONV_W - 1, width), lambda i, p: (i, 0, off + p))
    s_spec = pl.BlockSpec((per, nh, GLA_DK, GLA_DV), lambda i, p: (i, p, 0, 0))
    h_spec = pl.BlockSpec((per, hpg, SSD_HEAD_DIM, D_STATE), lambda i, p: (i, p, 0, 0))
    gla_specs = [
        pl.BlockSpec((block_rows, kw), rmap(0)), pl.BlockSpec((block_rows, kw), rmap(UA_K // kw)),
        pl.BlockSpec((block_rows, vw), rmap(UA_V // vw)), pl.BlockSpec((block_rows, vw), rmap(UA_R // vw)),
        pl.BlockSpec((block_rows, LANE), smap),
        pl.BlockSpec((LANE, kw), wmap(0)), pl.BlockSpec((1, kw), wmap(0)), pl.BlockSpec((1, GLA_DV), zmap),
        s_spec,
    ]
    ssd_specs = [
        pl.BlockSpec((block_rows, xw), rmap(UB_X // xw)), pl.BlockSpec((block_rows, sw), rmap(UB_B // sw)),
        pl.BlockSpec((block_rows, sw), rmap(UB_C // sw)), pl.BlockSpec((block_rows, xw), rmap(0)),
        pl.BlockSpec((block_rows, LANE), smap),
        pl.BlockSpec((CONV_W, xw), wmap(0)), pl.BlockSpec((CONV_W, sw), wmap(cvb)), pl.BlockSpec((CONV_W, sw), wmap(cvc)),
        pl.BlockSpec((1, xw), wmap(0)), pl.BlockSpec((1, sw), wmap(cvb)), pl.BlockSpec((1, sw), wmap(cvc)),
        pl.BlockSpec((1, LANE), zmap), pl.BlockSpec((1, LANE), zmap),
        pl.BlockSpec((ng, LANE, GROUP_WIDTH), lambda i, p: (p, 0, 0)),
        pl.BlockSpec((1, xw), wmap(0)), pl.BlockSpec((1, xw), wmap(0)),
        cs_spec(xw, 0), cs_spec(sw, cvb), cs_spec(sw, cvc),
        h_spec,
    ]
    assert len(gla_specs) == N_GLA_IN and len(ssd_specs) == N_SSD_IN
    wg, bg, gnw = gla_args
    conv_w, conv_b, dtb, alog, expand, dsk, nw = ssd_args
    return pl.pallas_call(
        functools.partial(_mixers_multi_kernel, seg_len=seq_rows),
        grid=(m // block_rows, GLA_HEADS // nh),
        in_specs=gla_specs + ssd_specs,
        out_specs=[pl.BlockSpec((block_rows, vw), rmap(0)), s_spec, pl.BlockSpec((block_rows, xw), rmap(0)), h_spec],
        out_shape=[jax.ShapeDtypeStruct((m, GLA_WIDTH), BF16),
                   jax.ShapeDtypeStruct((n_seq, GLA_HEADS, GLA_DK, GLA_DV), F32),
                   jax.ShapeDtypeStruct((m, SSD_WIDTH), BF16),
                   jax.ShapeDtypeStruct((n_seq, SSD_HEADS, SSD_HEAD_DIM, D_STATE), F32)],
        scratch_shapes=[pltpu.VMEM((8 * per + 8, cw), F32), pltpu.VMEM((8 * per, cw), F32),
                        pltpu.VMEM((block_rows, cw), F32)],
        compiler_params=pltpu.CompilerParams(dimension_semantics=("parallel", "parallel"),
                                             vmem_limit_bytes=VMEM_LIMIT),
        name="mixers_multi",
    )(ua, ua, ua, ua, usm, wg, bg, gnw, s0,
      ub, ub, ub, ub, usm, conv_w, conv_w, conv_w, conv_b, conv_b, conv_b, dtb, alog, expand, dsk, nw,
      conv0, conv0, conv0, h0)


def _matmul_kernel(x_ref, w_ref, o_ref, *wb_ref, act, w_t):
    w = w_ref[...].astype(BF16)
    if wb_ref:
        wb_ref[0][...] = w
    acc = _nt(x_ref[...], w) if w_t else _mm(x_ref[...], w)
    if act == "relu2":
        acc = jnp.square(jnp.maximum(acc, 0.0))
    o_ref[...] = acc.astype(o_ref.dtype)


def _matmul(x, w, *, tm, tn, out_dtype, act=None, name, n=None, emit=False, w_t=False, row0=0):
    m, k = x.shape
    n = w.shape[0 if w_t else 1] if n is None else n
    assert not emit or m == tm
    assert row0 == 0 or (w_t and row0 % 16 == 0)
    blk_spec = pl.BlockSpec((tn, k), lambda i, j: (j, 0)) if w_t else pl.BlockSpec((k, tn), lambda i, j: (0, j))
    w_spec = blk_spec if row0 == 0 else pl.BlockSpec((pl.Element(tn), pl.Element(k)),
                                                     lambda i, j: (pl.multiple_of(row0 + j * tn, 16), 0))
    out_specs = [pl.BlockSpec((tm, tn), lambda i, j: (i, j))]
    out_shape = [jax.ShapeDtypeStruct((m, n), out_dtype)]
    if emit:
        out_specs.append(blk_spec)
        out_shape.append(jax.ShapeDtypeStruct((n, k) if w_t else (k, n), BF16))
    res = pl.pallas_call(
        functools.partial(_matmul_kernel, act=act, w_t=w_t),
        grid=(m // tm, n // tn),
        in_specs=[pl.BlockSpec((tm, k), lambda i, j: (i, 0)), w_spec],
        out_specs=out_specs, out_shape=out_shape,
        compiler_params=pltpu.CompilerParams(dimension_semantics=("parallel", "parallel"),
                                             vmem_limit_bytes=VMEM_LIMIT),
        name=name,
    )(x, w)
    return res if emit else res[0]


def _small_proj_kernel(x_ref, w_ref, o_ref, xb_ref):
    xb = x_ref[...].astype(BF16)
    xb_ref[...] = xb
    o_ref[...] = _nt(xb, w_ref[...].astype(BF16))


def _small_proj(x, w, *, tm):
    m, k = x.shape
    return pl.pallas_call(
        _small_proj_kernel,
        grid=(m // tm,),
        in_specs=[pl.BlockSpec((tm, k), lambda i: (i, 0)), pl.BlockSpec((LANE, k), lambda i: (0, 0))],
        out_specs=[pl.BlockSpec((tm, LANE), lambda i: (i, 0)), pl.BlockSpec((tm, k), lambda i: (i, 0))],
        out_shape=[jax.ShapeDtypeStruct((m, LANE), F32), jax.ShapeDtypeStruct((m, k), BF16)],
        compiler_params=pltpu.CompilerParams(dimension_semantics=("parallel",), vmem_limit_bytes=VMEM_LIMIT),
        name="in_proj_small",
    )(x, w)


def _out_proj_kernel(a_ref, b_ref, wa_ref, wb_ref, res_ref, o_ref, *wo_refs):
    wa, wb = wa_ref[...], wb_ref[...]
    if wo_refs:
        wa, wb = wa.astype(BF16), wb.astype(BF16)
        wo_refs[0][...] = wa
        wo_refs[1][...] = wb
    o_ref[...] = ALPHA * res_ref[...] + _mm(a_ref[...], wa) + _mm(b_ref[...], wb)


def _out_proj(a, b, w_top, w_bot, res, *, tm, tn, emit=False):
    m, kh = a.shape
    n = w_top.shape[1]
    assert not emit or m == tm
    out_specs = [pl.BlockSpec((tm, tn), lambda i, j: (i, j))]
    out_shape = [jax.ShapeDtypeStruct((m, n), F32)]
    if emit:
        out_specs += [pl.BlockSpec((kh, tn), lambda i, j: (0, j))] * 2
        out_shape += [jax.ShapeDtypeStruct((kh, n), BF16)] * 2
    res_ = pl.pallas_call(
        _out_proj_kernel,
        grid=(m // tm, n // tn),
        in_specs=[pl.BlockSpec((tm, kh), lambda i, j: (i, 0)), pl.BlockSpec((tm, kh), lambda i, j: (i, 0)),
                  pl.BlockSpec((kh, tn), lambda i, j: (0, j)),
                  pl.BlockSpec((kh, tn), lambda i, j: (1 if emit else 0, j)),
                  pl.BlockSpec((tm, tn), lambda i, j: (i, j))],
        out_specs=out_specs, out_shape=out_shape,
        compiler_params=pltpu.CompilerParams(dimension_semantics=("parallel", "parallel"),
                                             vmem_limit_bytes=VMEM_LIMIT),
        name="out_proj",
    )(a, b, w_top, w_bot, res)
    return res_ if emit else res_[0]


def _down_proj_kernel(x_ref, w_ref, pre_ref, st_ref, g_ref, b_ref, o_ref, *rest):
    acc_ref = rest[-1]
    kk = pl.program_id(2)

    @pl.when(kk == 0)
    def _():
        st = st_ref[...]
        h = (pre_ref[...] - st[:, 0:1]) * st[:, 1:2] * g_ref[...] + b_ref[...]
        acc_ref[...] = ALPHA * h

    w = w_ref[...].astype(BF16)
    if len(rest) == 2:
        rest[0][...] = w
    acc_ref[...] += _mm(x_ref[...], w)

    @pl.when(kk == pl.num_programs(2) - 1)
    def _():
        o_ref[...] = acc_ref[...]


def _down_proj(x, w, pre, stats, g, b, *, tm, tn, tk, emit=False):
    m, k = x.shape
    n = w.shape[1]
    assert not emit or m == tm
    out_specs = [pl.BlockSpec((tm, tn), lambda i, j, kk: (i, j))]
    out_shape = [jax.ShapeDtypeStruct((m, n), F32)]
    if emit:
        out_specs.append(pl.BlockSpec((tk, tn), lambda i, j, kk: (kk, j)))
        out_shape.append(jax.ShapeDtypeStruct((k, n), BF16))
    res_ = pl.pallas_call(
        _down_proj_kernel,
        grid=(m // tm, n // tn, k // tk),
        in_specs=[pl.BlockSpec((tm, tk), lambda i, j, kk: (i, kk)), pl.BlockSpec((tk, tn), lambda i, j, kk: (kk, j)),
                  pl.BlockSpec((tm, tn), lambda i, j, kk: (i, j)), pl.BlockSpec((tm, LANE), lambda i, j, kk: (i, 0)),
                  pl.BlockSpec((1, tn), lambda i, j, kk: (0, j)), pl.BlockSpec((1, tn), lambda i, j, kk: (0, j))],
        out_specs=out_specs, out_shape=out_shape,
        scratch_shapes=[pltpu.VMEM((tm, tn), F32)],
        compiler_params=pltpu.CompilerParams(dimension_semantics=("parallel", "parallel", "arbitrary"),
                                             vmem_limit_bytes=VMEM_LIMIT),
        name="down_proj",
    )(x, w, pre, stats, g, b)
    return res_ if emit else res_[0]


def _layer_norm_kernel(x_ref, g_ref, b_ref, o_ref, *st_ref):
    x = x_ref[...]
    mu = jnp.mean(x, axis=-1, keepdims=True)
    xc = x - mu
    var = jnp.mean(jnp.square(xc), axis=-1, keepdims=True)
    rstd = lax.rsqrt(var + EPS)
    o_ref[...] = (xc * rstd * g_ref[...] + b_ref[...]).astype(o_ref.dtype)
    if st_ref:
        lane = lax.broadcasted_iota(jnp.int32, st_ref[0].shape, 1)
        st_ref[0][...] = jnp.where(lane == 0, mu, jnp.where(lane == 1, rstd, 0.0))


def _layer_norm(x, g, b, *, tm, dtype, stats=False):
    m, d = x.shape
    out_specs = [pl.BlockSpec((tm, d), lambda i: (i, 0))]
    out_shape = [jax.ShapeDtypeStruct((m, d), dtype)]
    if stats:
        out_specs.append(pl.BlockSpec((tm, LANE), lambda i: (i, 0)))
        out_shape.append(jax.ShapeDtypeStruct((m, LANE), F32))
    res = pl.pallas_call(
        _layer_norm_kernel,
        grid=(m // tm,),
        in_specs=[pl.BlockSpec((tm, d), lambda i: (i, 0)), pl.BlockSpec((1, d), lambda i: (0, 0)),
                  pl.BlockSpec((1, d), lambda i: (0, 0))],
        out_specs=out_specs, out_shape=out_shape,
        compiler_params=pltpu.CompilerParams(dimension_semantics=("parallel",), vmem_limit_bytes=VMEM_LIMIT),
        name="layer_norm",
    )(x, g, b)
    return res if stats else res[0]


def _expand_matrix():
    e = np.zeros((SSD_GROUPS, LANE, GROUP_WIDTH), np.float32)
    for g in range(SSD_GROUPS):
        for j in range(HEADS_PER_GROUP):
            e[g, SM_DT + g * HEADS_PER_GROUP + j, j * SSD_HEAD_DIM:(j + 1) * SSD_HEAD_DIM] = 1.0
    return jnp.asarray(e, BF16)


def _lane_param(p):
    return jnp.zeros((1, LANE), F32).at[0, SM_DT:SM_DT + SSD_HEADS].set(p.astype(F32))


def _dense_tail(x, og, ys, prm, tm, emit):
    wb = dict(prm)
    tn = 512 if emit else 1024
    w_top, w_bot = (prm["w_out"], prm["w_out"]) if emit else prm["w_out"]
    pre1 = _out_proj(og, ys, w_top, w_bot, x, tm=tm, tn=tn, emit=emit)
    if emit:
        pre1, wt, wbt = pre1
        wb["w_out"] = (wt, wbt)
    hb, stats = _layer_norm(pre1, prm["ln1_g"], prm["ln1_b"], tm=min(tm, 512), dtype=BF16, stats=True)
    hid = _matmul(hb, prm["w_up"], tm=tm, tn=tn, out_dtype=BF16, act="relu2", name="mlp_up", emit=emit)
    if emit:
        hid, wb["w_up"] = hid
    pre2 = _down_proj(hid, prm["w_down"], pre1, stats, prm["ln1_g"], prm["ln1_b"], tm=tm, tn=1024,
                      tk=2048, emit=emit)
    if emit:
        pre2, wb["w_down"] = pre2
    out = _layer_norm(pre2, prm["ln2_g"], prm["ln2_b"], tm=min(tm, 512), dtype=F32)
    return out, wb


def kernel(x_prompt, x_sample, state_gla, state_ssm, state_conv, meta_tokens, w_in, w_gk_up, b_gk, gla_norm_w,
           conv_w, conv_b, dt_bias, a_log, d_skip, ssd_norm_w, w_out, ln1_g, ln1_b, w_up, w_down, ln2_g, ln2_b):
    bp, seq, d = x_prompt.shape
    bs, dec = x_sample.shape[:2]
    assert w_in.shape[0] == DEPTH and seq % CHUNK == 0 and dec == 4 and N_META <= CHUNK
    assert bs % SEQS_PER_BLOCK == 0
    wi = w_in[0]
    o_g = UA_WIDTH
    o_z = o_g + GATE_RANK
    o_dt = o_z + UB_WIDTH
    wt = wi.T
    w_small = jnp.concatenate([wt[o_g:o_z], wt[o_dt:], jnp.zeros((LANE - GATE_RANK - SSD_HEADS, d), F32)], axis=0)
    wg = jnp.concatenate([w_gk_up[0], jnp.zeros((LANE - GATE_RANK, GLA_QK), F32)], axis=0).astype(BF16)
    prm = dict(
        w_out=w_out[0], w_up=w_up[0], w_down=w_down[0],
        ln1_g=ln1_g[0][None], ln1_b=ln1_b[0][None], ln2_g=ln2_g[0][None], ln2_b=ln2_b[0][None],
    )
    gla_args = (wg, b_gk[0][None], gla_norm_w[0][None])
    ssd_args = (conv_w[0], conv_b[0][None], _lane_param(dt_bias[0]), _lane_param(a_log[0]), _expand_matrix(),
                jnp.repeat(d_skip[0], SSD_HEAD_DIM)[None, :], ssd_norm_w[0][None])
    tail = CONV_W - 1

    def in_proj(x, tm, w_ab):
        usm, xb = _small_proj(x, w_small, tm=512 if tm % 512 == 0 else tm)
        if w_ab is None:
            ua, w_a = _matmul(xb, wt, tm=tm, tn=512, out_dtype=F32, name="in_proj_gla", n=UA_WIDTH, emit=True,
                              w_t=True)
            ub, w_b = _matmul(xb, wt, tm=tm, tn=512, out_dtype=F32, name="in_proj_ssd", n=UB_WIDTH, emit=True,
                              w_t=True, row0=o_z)
        else:
            w_a, w_b = w_ab
            ua = _matmul(xb, w_a, tm=tm, tn=1024, out_dtype=F32, name="in_proj_gla", w_t=True)
            ub = _matmul(xb, w_b, tm=tm, tn=1024, out_dtype=F32, name="in_proj_ssd", w_t=True)
        return ua, ub, usm, (w_a, w_b)

    ns = bs * dec
    lead_null = CHUNK - N_META
    xs = x_sample.reshape(ns, d)
    xsm = jnp.concatenate([xs, jnp.zeros((lead_null, d), F32), meta_tokens], axis=0)
    ua_s, ub_s, usm_s, w_ab = in_proj(xsm, ns + CHUNK, None)
    og_s, gla_s, ys_s, ssm_s = _mixers_multi(ua_s, ub_s, usm_s, gla_args, ssd_args, state_gla[0], state_conv[0],
                                             state_ssm[0], n_seq=bs, seq_rows=dec)
    y_sample, prm_b = _dense_tail(xs, og_s, ys_s, prm, ns, True)
    y_sample = y_sample.reshape(bs, dec, d)
    conv_s = ub_s[:ns, UB_X:].reshape(bs, dec, CONV_DIM)[:, dec - tail:]

    _, gla_m = _gla_scan(ua_s, usm_s, *gla_args, jnp.zeros((1, GLA_HEADS, GLA_DK, GLA_DV), F32),
                         n_seq=1, seq_rows=CHUNK, block_rows=CHUNK, lead_null=lead_null, row0=ns)
    _, ssm_m = _ssd_scan(ub_s, usm_s, *ssd_args, jnp.zeros((1, 8, CONV_DIM), F32),
                         jnp.zeros((1, SSD_HEADS, SSD_HEAD_DIM, D_STATE), F32),
                         n_seq=1, seq_rows=CHUNK, block_rows=CHUNK, lead_null=lead_null, row0=ns)
    conv_m = ub_s[None, ns + CHUNK - 8:, UB_X:]

    xp = x_prompt.reshape(bp * seq, d)
    ua_p, ub_p, usm_p, _ = in_proj(xp, 1024, w_ab)
    og_p, gla_p = _gla_scan(ua_p, usm_p, *gla_args, gla_m, n_seq=bp, seq_rows=seq, block_rows=256)
    ys_p, ssm_p = _ssd_scan(ub_p, usm_p, *ssd_args, conv_m, ssm_m, n_seq=bp, seq_rows=seq, block_rows=256)
    y_prompt, _ = _dense_tail(xp, og_p, ys_p, prm_b, 1024, False)
    y_prompt = y_prompt.reshape(bp, seq, d)
    conv_p = jnp.stack([ub_p[(b + 1) * seq - tail:(b + 1) * seq, UB_X:] for b in range(bp)])

    return (y_prompt, y_sample, gla_p[None], ssm_p[None], conv_p[None], gla_s[None], ssm_s[None], conv_s[None])
```
